```python
import math
import jax
import jax.numpy as jnp
from jax import lax
import numpy as np

D_MODEL = 1024
BATCH = 1
SEQ = 16384
DEPTH = 4

GRID_W = 64
CTX_LEN = 256
N_MIXERS = 3
EPS = 1e-6

SSM_INNER = 2 * D_MODEL
SSM_HEAD_DIM = 64
SSM_HEADS = SSM_INNER // SSM_HEAD_DIM
SSM_GROUPS = 8
SSM_HPG = SSM_HEADS // SSM_GROUPS
SSM_STATE = 128
SSM_CONV = 7
SSD_CHUNK = 128
SSM_PROJ = 2 * SSM_INNER + 2 * SSM_GROUPS * SSM_STATE + 2 * SSM_HEADS
DT_MIN = 1e-3
DT_MAX = 1e-1

SGU_INNER = 2 * D_MODEL
SGU_GROUPS = 8
TOKEN_CHUNK = 128

NA_HEAD_DIM = 64
NA_HEADS = D_MODEL // NA_HEAD_DIM
NA_ROW_WIN = 8
NA_COL_WIN = 16

FFN_HIDDEN = 2816
FFN_CONV = 3

kernel_name = "hybrid_ssd_sgu_natten_dit"


def rmsnorm(x, w):
    xf = x.astype(jnp.float32)
    y = xf * lax.rsqrt(jnp.mean(xf * xf, axis=-1, keepdims=True) + EPS)
    return (y * w.astype(jnp.float32)).astype(x.dtype)


def layernorm(x, w, b):
    xf = x.astype(jnp.float32)
    mu = jnp.mean(xf, axis=-1, keepdims=True)
    xc = xf - mu
    y = xc * lax.rsqrt(jnp.mean(xc * xc, axis=-1, keepdims=True) + EPS)
    return (y * w.astype(jnp.float32) + b.astype(jnp.float32)).astype(x.dtype)


def modulate(x, w, shift, scale):
    return rmsnorm(x, w) * (1.0 + scale) + shift


def dwconv(x, w, b):
    ch = x.shape[-1]
    y = lax.conv_general_dilated(x, w[:, None, :].astype(x.dtype), window_strides=(1,), padding='SAME',
                                 dimension_numbers=('NWC', 'WIO', 'NWC'), feature_group_count=ch)
    return y + b.astype(x.dtype)


def ssd_chunked(xdt, dA, bg, cg, h0):
    b, L = xdt.shape[:2]
    nc = L // SSD_CHUNK
    xc = xdt.reshape(b, nc, SSD_CHUNK, SSM_GROUPS, SSM_HPG, SSM_HEAD_DIM)
    ac = dA.reshape(b, nc, SSD_CHUNK, SSM_GROUPS, SSM_HPG)
    bc = bg.reshape(b, nc, SSD_CHUNK, SSM_GROUPS, SSM_STATE)
    cc = cg.reshape(b, nc, SSD_CHUNK, SSM_GROUPS, SSM_STATE)
    acum = jnp.cumsum(ac, axis=2)
    tri = jnp.tril(jnp.ones((SSD_CHUNK, SSD_CHUNK), dtype=bool))
    seg = acum[:, :, :, None] - acum[:, :, None, :]
    decay_in = jnp.exp(jnp.where(tri[None, None, :, :, None, None], seg, -jnp.inf))
    cb = jnp.einsum('bclgn,bcsgn->bclsg', cc, bc)
    y_diag = jnp.einsum('bclsg,bclsgr,bcsgrp->bclgrp', cb, decay_in, xc)
    decay_out = jnp.exp(acum[:, :, -1:] - acum)
    states = jnp.einsum('bclgn,bclgr,bclgrp->bcgrpn', bc, decay_out, xc)
    chunk_decay = jnp.exp(acum[:, :, -1])

    def carry(h, inp):
        dec, st = inp
        return h * dec[..., None, None] + st, h

    h_final, h_enter = lax.scan(carry, h0, (jnp.moveaxis(chunk_decay, 1, 0), jnp.moveaxis(states, 1, 0)))
    h_enter = jnp.moveaxis(h_enter, 0, 1)
    y_off = jnp.einsum('bclgn,bcgrpn,bclgr->bclgrp', cc, h_enter, jnp.exp(acum))
    y = (y_diag + y_off).reshape(b, L, SSM_GROUPS, SSM_HPG, SSM_HEAD_DIM)
    return y, h_final


def ssd_direction(xs, bm, cm, dt_raw, a_log_d, dt_bias_d, d_skip_d, h0, reverse):
    b, L, _ = xs.shape
    x4 = xs.astype(jnp.float32).reshape(b, L, SSM_GROUPS, SSM_HPG, SSM_HEAD_DIM)
    bg = bm.astype(jnp.float32).reshape(b, L, SSM_GROUPS, SSM_STATE)
    cg = cm.astype(jnp.float32).reshape(b, L, SSM_GROUPS, SSM_STATE)
    dt = jax.nn.softplus(dt_raw.astype(jnp.float32) + dt_bias_d.astype(jnp.float32))
    dt = dt.reshape(b, L, SSM_GROUPS, SSM_HPG)
    dA = dt * (-jnp.exp(a_log_d.astype(jnp.float32))).reshape(SSM_GROUPS, SSM_HPG)
    xdt = x4 * dt[..., None]
    if reverse:
        xdt, dA, bg, cg = jnp.flip(xdt, 1), jnp.flip(dA, 1), jnp.flip(bg, 1), jnp.flip(cg, 1)
    y, h_final = ssd_chunked(xdt, dA, bg, cg, h0)
    if reverse:
        y = jnp.flip(y, 1)
    y = y + d_skip_d.astype(jnp.float32).reshape(SSM_GROUPS, SSM_HPG)[:, :, None] * x4
    return y, h_final


def gated_group_rmsnorm(y, z, w):
    g = y * jax.nn.silu(z.astype(jnp.float32))
    gg = g.reshape(g.shape[:-1] + (SSM_GROUPS, SSM_INNER // SSM_GROUPS))
    gg = gg * lax.rsqrt(jnp.mean(gg * gg, axis=-1, keepdims=True) + EPS)
    return gg.reshape(g.shape) * w.astype(jnp.float32)


def ssd_mixer(h_ctx, h_lat, w_in, conv_w, conv_b, a_log, dt_bias, d_skip, norm_w, w_out, need_ctx):
    gn = SSM_GROUPS * SSM_STATE

    def project(h):
        z, xbc, dt = jnp.split(h @ w_in, [SSM_INNER, 2 * SSM_INNER + 2 * gn], axis=-1)
        xbc = jax.nn.silu(dwconv(xbc, conv_w, conv_b))
        xs, bm, cm = jnp.split(xbc, [SSM_INNER, SSM_INNER + gn], axis=-1)
        return z, xs, bm, cm, dt[..., :SSM_HEADS], dt[..., SSM_HEADS:]

    z_c, x_c, b_c, c_c, dtf_c, dtb_c = project(h_ctx)
    z_l, x_l, b_l, c_l, dtf_l, dtb_l = project(h_lat)
    h0 = jnp.zeros((h_lat.shape[0], SSM_GROUPS, SSM_HPG, SSM_HEAD_DIM, SSM_STATE), jnp.float32)
    yf_c, hf = ssd_direction(x_c, b_c, c_c, dtf_c, a_log[0], dt_bias[0], d_skip[0], h0, False)
    yb_c, hb = ssd_direction(x_c, b_c, c_c, dtb_c, a_log[1], dt_bias[1], d_skip[1], h0, True)
    yf_l, _ = ssd_direction(x_l, b_l, c_l, dtf_l, a_log[0], dt_bias[0], d_skip[0], hf, False)
    yb_l, _ = ssd_direction(x_l, b_l, c_l, dtb_l, a_log[1], dt_bias[1], d_skip[1], hb, True)

    def finish(yf, yb, z, h):
        y = (yf + yb).reshape(z.shape)
        return gated_group_rmsnorm(y, z, norm_w).astype(h.dtype) @ w_out

    y_ctx = finish(yf_c, yb_c, z_c, h_ctx) if need_ctx else None
    return y_ctx, finish(yf_l, yb_l, z_l, h_lat)


def sgu_mixer(h, w_in, ln_w, ln_b, w_s, b_s, w_out):
    z = jax.nn.gelu(h @ w_in)
    u, v = jnp.split(z, 2, axis=-1)
    v = layernorm(v, ln_w, ln_b)
    b, L, e = v.shape
    vc = v.reshape(b, L // TOKEN_CHUNK, TOKEN_CHUNK, SGU_GROUPS, e // SGU_GROUPS)
    sv = jnp.einsum('gts,bcsgd->bctgd', w_s, vc) + b_s.T[:, :, None]
    return (u * sv.reshape(b, L, e)) @ w_out


def na_mixer(h_ctx, h_lat, w_qkv, q_norm, k_norm, rpb, w_out, need_ctx):
    scale = NA_HEAD_DIM ** -0.5

    def heads(h):
        b, L, _ = h.shape
        q, k, v = jnp.split(h @ w_qkv, 3, axis=-1)
        shp = (b, L, NA_HEADS, NA_HEAD_DIM)
        return rmsnorm(q.reshape(shp), q_norm), rmsnorm(k.reshape(shp), k_norm), v.reshape(shp)

    q_c, k_c, v_c = heads(h_ctx)
    q_l, k_l, v_l = heads(h_lat)
    b, L = h_lat.shape[:2]
    rows = L // GRID_W
    wr = min(NA_ROW_WIN, rows)
    y_ctx = None
    if need_ctx:
        s = jnp.einsum('bqhd,bkhd->bhqk', q_c, k_c).astype(jnp.float32) * scale
        p = jax.nn.softmax(s, axis=-1).astype(v_c.dtype)
        y_ctx = jnp.einsum('bhqk,bkhd->bqhd', p, v_c).reshape(b, h_ctx.shape[1], D_MODEL) @ w_out
    col = jnp.arange(GRID_W)
    col_start = jnp.clip(col - NA_COL_WIN // 2, 0, GRID_W - NA_COL_WIN)
    in_win = (col[None, :] >= col_start[:, None]) & (col[None, :] < col_start[:, None] + NA_COL_WIN)
    col_idx = jnp.clip(col[None, :] - col[:, None] + NA_COL_WIN - 1, 0, 2 * NA_COL_WIN - 2)
    col_bias = rpb.astype(jnp.float32)[:, :, col_idx]
    col_bias = jnp.where(in_win, col_bias, -jnp.inf)
    row_start = jnp.clip(jnp.arange(rows) - wr // 2, 0, rows - wr)
    q_g = q_l.reshape(b, rows, GRID_W, NA_HEADS, NA_HEAD_DIM)
    k_g = k_l.reshape(b, rows, GRID_W, NA_HEADS, NA_HEAD_DIM)
    v_g = v_l.reshape(b, rows, GRID_W, NA_HEADS, NA_HEAD_DIM)
    n_win = wr * GRID_W

    def row_block(r):
        rs = row_start[r]
        q = q_g[:, r]
        kb = lax.dynamic_slice_in_dim(k_g, rs, wr, axis=1)
        vb = lax.dynamic_slice_in_dim(v_g, rs, wr, axis=1)
        row_idx = rs + jnp.arange(wr) - r + NA_ROW_WIN - 1
        bias = jnp.transpose(col_bias[:, row_idx], (0, 2, 1, 3))
        s_win = jnp.einsum('bqhd,bikhd->bhqik', q, kb).astype(jnp.float32) * scale + bias[None]
        s_ctx = jnp.einsum('bqhd,bkhd->bhqk', q, k_c).astype(jnp.float32) * scale
        s = jnp.concatenate([s_win.reshape(b, NA_HEADS, GRID_W, n_win), s_ctx], axis=-1)
        p = jax.nn.softmax(s, axis=-1).astype(v_l.dtype)
        p_win = p[..., :n_win].reshape(b, NA_HEADS, GRID_W, wr, GRID_W)
        return (jnp.einsum('bhqik,bikhd->bqhd', p_win, vb)
                + jnp.einsum('bhqk,bkhd->bqhd', p[..., n_win:], v_c))

    o = lax.map(row_block, jnp.arange(rows))
    y_lat = jnp.moveaxis(o, 0, 1).reshape(b, L, D_MODEL) @ w_out
    return y_ctx, y_lat


def conv_ffn(h, w_up, conv_w, conv_b, w_down):
    a = dwconv(h @ w_up, conv_w, conv_b)
    g, u = jnp.split(a, 2, axis=-1)
    return (jax.nn.silu(g) * u) @ w_down


def setup_inputs(seed: int = 0) -> dict:
    key = jax.random.key(seed)
    ks = iter(jax.random.split(key, 48))
    n_a = len(range(0, DEPTH, N_MIXERS))
    n_b = len(range(1, DEPTH, N_MIXERS))
    n_c = len(range(2, DEPTH, N_MIXERS))

    def nrm(shape, scale):
        return jax.random.normal(next(ks), shape, jnp.float32) * scale

    def gain(shape):
        return 1.0 + nrm(shape, 0.02)

    u = jax.random.uniform(next(ks), (n_a, 2, SSM_HEADS), jnp.float32)
    dt = jnp.exp(u * (math.log(DT_MAX) - math.log(DT_MIN)) + math.log(DT_MIN))
    ssm_dt_bias = dt + jnp.log(-jnp.expm1(-dt))
    ssm_a_log = jnp.log(jax.random.uniform(next(ks), (n_a, 2, SSM_HEADS), jnp.float32, 1.0, 16.0))
    return {
        'x': nrm((BATCH, SEQ, D_MODEL), 1.0),
        'c': nrm((BATCH, D_MODEL), 1.0),
        'ctx': nrm((BATCH, CTX_LEN, D_MODEL), 1.0),
        'c_ctx': nrm((D_MODEL,), 1.0),
        'norm_w': gain((DEPTH, 2, D_MODEL)),
        'w_mod': nrm((DEPTH, D_MODEL, 6 * D_MODEL), 0.5 * D_MODEL ** -0.5),
        'b_mod': nrm((DEPTH, 6 * D_MODEL), 0.02),
        'ssm_w_in': nrm((n_a, D_MODEL, SSM_PROJ), D_MODEL ** -0.5),
        'ssm_conv_w': nrm((n_a, SSM_CONV, SSM_INNER + 2 * SSM_GROUPS * SSM_STATE), SSM_CONV ** -0.5),
        'ssm_conv_b': nrm((n_a, SSM_INNER + 2 * SSM_GROUPS * SSM_STATE), 0.02),
        'ssm_a_log': ssm_a_log,
        'ssm_dt_bias': ssm_dt_bias,
        'ssm_d_skip': gain((n_a, 2, SSM_HEADS)),
        'ssm_norm_w': gain((n_a, SSM_INNER)),
        'ssm_w_out': nrm((n_a, SSM_INNER, D_MODEL), SSM_INNER ** -0.5),
        'sgu_w_in': nrm((n_b, D_MODEL, 2 * SGU_INNER), D_MODEL ** -0.5),
        'sgu_ln_w': gain((n_b, SGU_INNER)),
        'sgu_ln_b': nrm((n_b, SGU_INNER), 0.02),
        'sgu_w_s': nrm((n_b, SGU_GROUPS, TOKEN_CHUNK, TOKEN_CHUNK), 0.5 * TOKEN_CHUNK ** -0.5),
        'sgu_b_s': 1.0 + nrm((n_b, SGU_GROUPS, TOKEN_CHUNK), 0.02),
        'sgu_w_out': nrm((n_b, SGU_INNER, D_MODEL), SGU_INNER ** -0.5),
        'na_w_qkv': nrm((n_c, D_MODEL, 3 * D_MODEL), D_MODEL ** -0.5),
        'na_q_norm': gain((n_c, NA_HEAD_DIM)),
        'na_k_norm': gain((n_c, NA_HEAD_DIM)),
        'na_rpb': nrm((n_c, NA_HEADS, 2 * NA_ROW_WIN - 1, 2 * NA_COL_WIN - 1), 0.1),
        'na_w_out': nrm((n_c, D_MODEL, D_MODEL), D_MODEL ** -0.5),
        'ffn_w_up': nrm((DEPTH, D_MODEL, 2 * FFN_HIDDEN), D_MODEL ** -0.5),
        'ffn_conv_w': nrm((DEPTH, FFN_CONV, 2 * FFN_HIDDEN), FFN_CONV ** -0.5),
        'ffn_conv_b': nrm((DEPTH, 2 * FFN_HIDDEN), 0.02),
        'ffn_w_down': nrm((DEPTH, FFN_HIDDEN, D_MODEL), FFN_HIDDEN ** -0.5),
    }


def reference(x, c, ctx, c_ctx, norm_w, w_mod, b_mod,
              ssm_w_in, ssm_conv_w, ssm_conv_b, ssm_a_log, ssm_dt_bias, ssm_d_skip, ssm_norm_w, ssm_w_out,
              sgu_w_in, sgu_ln_w, sgu_ln_b, sgu_w_s, sgu_b_s, sgu_w_out,
              na_w_qkv, na_q_norm, na_k_norm, na_rpb, na_w_out,
              ffn_w_up, ffn_conv_w, ffn_conv_b, ffn_w_down):
    silu_c = jax.nn.silu(c)
    silu_cc = jax.nn.silu(c_ctx)
    x_lat, x_ctx = x, ctx
    for i in range(DEPTH):
        kind, j = i % N_MIXERS, i // N_MIXERS
        need_ctx = i < DEPTH - 1
        mod_lat = jnp.split((silu_c @ w_mod[i] + b_mod[i])[:, None, :], 6, axis=-1)
        mod_ctx = jnp.split(silu_cc @ w_mod[i] + b_mod[i], 6, axis=-1)
        h_lat = modulate(x_lat, norm_w[i, 0], mod_lat[0], mod_lat[1])
        h_ctx = modulate(x_ctx, norm_w[i, 0], mod_ctx[0], mod_ctx[1]) if (need_ctx or kind != 1) else None
        if kind == 0:
            y_ctx, y_lat = ssd_mixer(h_ctx, h_lat, ssm_w_in[j], ssm_conv_w[j], ssm_conv_b[j], ssm_a_log[j],
                                     ssm_dt_bias[j], ssm_d_skip[j], ssm_norm_w[j], ssm_w_out[j], need_ctx)
        elif kind == 1:
            y_lat = sgu_mixer(h_lat, sgu_w_in[j], sgu_ln_w[j], sgu_ln_b[j], sgu_w_s[j], sgu_b_s[j], sgu_w_out[j])
            y_ctx = (sgu_mixer(h_ctx, sgu_w_in[j], sgu_ln_w[j], sgu_ln_b[j], sgu_w_s[j], sgu_b_s[j], sgu_w_out[j])
                     if need_ctx else None)
        else:
            y_ctx, y_lat = na_mixer(h_ctx, h_lat, na_w_qkv[j], na_q_norm[j], na_k_norm[j], na_rpb[j],
                                    na_w_out[j], need_ctx)
        x_lat = x_lat + mod_lat[2] * y_lat
        h_lat = modulate(x_lat, norm_w[i, 1], mod_lat[3], mod_lat[4])
        x_lat = x_lat + mod_lat[5] * conv_ffn(h_lat, ffn_w_up[i], ffn_conv_w[i], ffn_conv_b[i], ffn_w_down[i])
        if need_ctx:
            x_ctx = x_ctx + mod_ctx[2] * y_ctx
            h_ctx = modulate(x_ctx, norm_w[i, 1], mod_ctx[3], mod_ctx[4])
            x_ctx = x_ctx + mod_ctx[5] * conv_ffn(h_ctx, ffn_w_up[i], ffn_conv_w[i], ffn_conv_b[i], ffn_w_down[i])
    return x_lat
```

```python
import functools
import math

import numpy as np
import jax
import jax.numpy as jnp
from jax import lax
from jax.experimental import pallas as pl
from jax.experimental.pallas import tpu as pltpu

F32 = jnp.float32
BF16 = jnp.bfloat16
HIGHEST = lax.Precision.HIGHEST

D_MODEL = 1024
DEPTH = 4
N_MIXERS = 3
EPS = 1e-6
GRID_W = 64
SSM_INNER = 2 * D_MODEL
SSM_HEAD_DIM = 64
SSM_HEADS = SSM_INNER // SSM_HEAD_DIM
SSM_GROUPS = 8
SSM_HPG = SSM_HEADS // SSM_GROUPS
SSM_STATE = 128
SSM_CONV = 7
SSD_CHUNK = 128
SSM_GN = SSM_GROUPS * SSM_STATE
SSM_ZXBC = 2 * SSM_INNER + 2 * SSM_GN
SGU_INNER = 2 * D_MODEL
SGU_GROUPS = 8
TOKEN_CHUNK = 128
NA_HEAD_DIM = 64
NA_HEADS = D_MODEL // NA_HEAD_DIM
NA_ROW_WIN = 8
NA_COL_WIN = 16
FFN_HIDDEN = 2816
FFN_CONV = 3

LANES = 128
BF16_SUBLANES = 16
VMEM_LIMIT = 56 * 1024 * 1024

HALO = BF16_SUBLANES
FFN_CHUNK = 256
NA_RB = 4
NA_WR = NA_RB + NA_ROW_WIN


def _cparams(sem):
    return pltpu.CompilerParams(dimension_semantics=sem, vmem_limit_bytes=VMEM_LIMIT)


def _row_tile(n, pref):
    t = min(n, pref)
    assert n % t == 0
    return t


def _sigmoid(v):
    return 1.0 / (1.0 + jnp.exp(-v))


def _silu(v):
    return v * _sigmoid(v)


def _modulate(x, mul, shift):
    ms = jnp.mean(x * x, axis=-1, keepdims=True)
    return x * lax.rsqrt(ms + EPS) * mul + shift


def _lane_lt(shape, n):
    return lax.broadcasted_iota(jnp.int32, shape, len(shape) - 1) < n


def _mod_kernel(c_ref, w_ref, b_ref, o_ref):
    s = _silu(c_ref[...])
    o_ref[0] = jnp.dot(s, w_ref[0], preferred_element_type=F32, precision=HIGHEST) + b_ref[0]


def _mod_vectors(cond, w_mod, b_mod):
    depth, d, n = w_mod.shape
    tn = 1536
    return pl.pallas_call(
        _mod_kernel,
        grid=(depth, n // tn),
        in_specs=[
            pl.BlockSpec((8, d), lambda i, j: (0, 0)),
            pl.BlockSpec((1, d, tn), lambda i, j: (i, 0, j)),
            pl.BlockSpec((1, 1, tn), lambda i, j: (i, 0, j)),
        ],
        out_specs=pl.BlockSpec((1, 8, tn), lambda i, j: (i, 0, j)),
        out_shape=jax.ShapeDtypeStruct((depth, 8, n), F32),
        compiler_params=_cparams(("arbitrary", "arbitrary")),
        name="mod_vectors",
    )(cond, w_mod, b_mod.reshape(depth, 1, n))


def _gelu_tanh(v):
    c = math.sqrt(2.0 / math.pi)
    return v * (0.5 * (1.0 + jnp.tanh(c * (v + 0.044715 * (v * v * v)))))


def _head_rmsnorm(blk, nw):
    lo = _lane_lt(blk.shape, NA_HEAD_DIM)
    sq = blk * blk
    s_lo = jnp.sum(jnp.where(lo, sq, 0.0), axis=-1, keepdims=True)
    s_hi = jnp.sum(jnp.where(lo, 0.0, sq), axis=-1, keepdims=True)
    r_lo = lax.rsqrt(s_lo * (1.0 / NA_HEAD_DIM) + EPS)
    r_hi = lax.rsqrt(s_hi * (1.0 / NA_HEAD_DIM) + EPS)
    return blk * jnp.where(lo, r_lo, r_hi) * nw


def _modmm_kernel(x_ref, mul_ref, sh_ref, w_ref, *rest, mode, tn):
    if mode == "qkv":
        nw_ref, o_ref, h_scr = rest
    else:
        o_ref, h_scr = rest
    j = pl.program_id(1)

    @pl.when(j == 0)
    def _():
        h_scr[...] = _modulate(x_ref[...], mul_ref[...], sh_ref[...]).astype(BF16)

    if mode == "gelu":
        acc = jnp.dot(h_scr[...], w_ref[...], preferred_element_type=F32)
        o_ref[...] = _gelu_tanh(acc).astype(o_ref.dtype)
    else:
        n_norm = 2 * D_MODEL // tn

        @pl.when(j < n_norm)
        def _():
            acc = jnp.dot(h_scr[...], w_ref[...], preferred_element_type=F32)
            for b in range(tn // LANES):
                sl = slice(b * LANES, (b + 1) * LANES)
                o_ref[:, sl] = _head_rmsnorm(acc[:, sl], nw_ref[:, sl]).astype(o_ref.dtype)

        @pl.when(j >= n_norm)
        def _():
            acc = jnp.dot(h_scr[...], w_ref[...], preferred_element_type=F32)
            o_ref[...] = acc.astype(o_ref.dtype)


def _modmm(x, mul, shift, w, mode, nw=None):
    rows, d = x.shape
    n = w.shape[1]
    tm = _row_tile(rows, 1024)
    tn = 512
    in_specs = [
        pl.BlockSpec((tm, d), lambda i, j: (i, 0)),
        pl.BlockSpec((1, d), lambda i, j: (0, 0)),
        pl.BlockSpec((1, d), lambda i, j: (0, 0)),
        pl.BlockSpec((d, tn), lambda i, j: (0, j)),
    ]
    args = [x, mul, shift, w]
    if mode == "qkv":
        in_specs.append(pl.BlockSpec((1, tn), lambda i, j: (0, j)))
        args.append(nw)
    return pl.pallas_call(
        functools.partial(_modmm_kernel, mode=mode, tn=tn),
        grid=(rows // tm, n // tn),
        in_specs=in_specs,
        out_specs=pl.BlockSpec((tm, tn), lambda i, j: (i, j)),
        out_shape=jax.ShapeDtypeStruct((rows, n), BF16),
        scratch_shapes=[pltpu.VMEM((tm, d), BF16)],
        compiler_params=_cparams(("arbitrary", "arbitrary")),
        name="modmm_" + mode,
    )(*args)


def _mmres_kernel(a_ref, w_ref, x_ref, g_ref, o_ref):
    y = jnp.dot(a_ref[...], w_ref[...], preferred_element_type=F32)
    o_ref[...] = x_ref[...] + g_ref[...] * y


def _mm_res(a, w, x, gate):
    rows, k = a.shape
    d = w.shape[1]
    tm = _row_tile(rows, 1024)
    return pl.pallas_call(
        _mmres_kernel,
        grid=(rows // tm,),
        in_specs=[
            pl.BlockSpec((tm, k), lambda i: (i, 0)),
            pl.BlockSpec((k, d), lambda i: (0, 0)),
            pl.BlockSpec((tm, d), lambda i: (i, 0)),
            pl.BlockSpec((1, d), lambda i: (0, 0)),
        ],
        out_specs=pl.BlockSpec((tm, d), lambda i: (i, 0)),
        out_shape=jax.ShapeDtypeStruct((rows, d), F32),
        compiler_params=_cparams(("arbitrary",)),
        name="mm_res",
    )(a, w, x, gate)


def _fill_h_with_halo(h_scr, xp_ref, x_ref, xn_ref, mul, sh, tm, nblk):
    i = pl.program_id(0)
    h_scr[HALO:HALO + tm, :] = _modulate(x_ref[...], mul, sh).astype(BF16)
    hp = jnp.where(i > 0, _modulate(xp_ref[...], mul, sh), 0.0)
    hn = jnp.where(i < nblk - 1, _modulate(xn_ref[...], mul, sh), 0.0)
    h_scr[0:HALO, :] = hp.astype(BF16)
    h_scr[HALO + tm:HALO + tm + HALO, :] = hn.astype(BF16)


def _halo_specs(tm, d, rows):
    per = tm // HALO
    last = rows // HALO - 1
    return [
        pl.BlockSpec((HALO, d), lambda i, *_: (jnp.maximum(i * per - 1, 0), 0)),
        pl.BlockSpec((tm, d), lambda i, *_: (i, 0)),
        pl.BlockSpec((HALO, d), lambda i, *_: (jnp.minimum((i + 1) * per, last), 0)),
    ]


def _dwconv_from_scratch(acc_scr, cw_ref, cb_ref, taps, tm):
    base = HALO - taps // 2
    y = cb_ref[...] + cw_ref[0:1, :] * acc_scr[pl.ds(base, tm), :]
    for k in range(1, taps):
        y = y + cw_ref[k:k + 1, :] * acc_scr[pl.ds(base + k, tm), :]
    return y


def _ssm_in_kernel(xp_ref, x_ref, xn_ref, mul_ref, sh_ref, w_ref, wdt_ref, cw_ref, cb_ref,
                   o_ref, dt_ref, h_scr, acc_scr, *, tm, nblk, n_plain):
    j = pl.program_id(1)

    @pl.when(j == 0)
    def _():
        _fill_h_with_halo(h_scr, xp_ref, x_ref, xn_ref, mul_ref[...], sh_ref[...], tm, nblk)
        dt_ref[...] = jnp.dot(h_scr[HALO:HALO + tm, :], wdt_ref[...], preferred_element_type=F32)

    @pl.when(j < n_plain)
    def _():
        acc = jnp.dot(h_scr[HALO:HALO + tm, :], w_ref[...], preferred_element_type=F32)
        o_ref[...] = acc.astype(o_ref.dtype)

    @pl.when(j >= n_plain)
    def _():
        acc_scr[...] = jnp.dot(h_scr[...], w_ref[...], preferred_element_type=F32)
        y = _dwconv_from_scratch(acc_scr, cw_ref, cb_ref, SSM_CONV, tm)
        o_ref[...] = _silu(y).astype(o_ref.dtype)


def _ssm_in(x, mul, shift, w_zxbc, w_dt, conv_w, conv_b):
    rows, d = x.shape
    n = w_zxbc.shape[1]
    tm = _row_tile(rows, 1024)
    tn = 512
    nblk = rows // tm
    ndt = w_dt.shape[1]
    in_specs = _halo_specs(tm, d, rows) + [
        pl.BlockSpec((1, d), lambda i, j: (0, 0)),
        pl.BlockSpec((1, d), lambda i, j: (0, 0)),
        pl.BlockSpec((d, tn), lambda i, j: (0, j)),
        pl.BlockSpec((d, ndt), lambda i, j: (0, 0)),
        pl.BlockSpec((SSM_CONV, tn), lambda i, j: (0, j)),
        pl.BlockSpec((1, tn), lambda i, j: (0, j)),
    ]
    return pl.pallas_call(
        functools.partial(_ssm_in_kernel, tm=tm, nblk=nblk, n_plain=SSM_INNER // tn),
        grid=(nblk, n // tn),
        in_specs=in_specs,
        out_specs=[
            pl.BlockSpec((tm, tn), lambda i, j: (i, j)),
            pl.BlockSpec((tm, ndt), lambda i, j: (i, 0)),
        ],
        out_shape=[
            jax.ShapeDtypeStruct((rows, n), BF16),
            jax.ShapeDtypeStruct((rows, ndt), F32),
        ],
        scratch_shapes=[
            pltpu.VMEM((tm + 2 * HALO, d), BF16),
            pltpu.VMEM((tm + 2 * HALO, tn), F32),
        ],
        compiler_params=_cparams(("arbitrary", "arbitrary")),
        name="ssm_in",
    )(x, x, x, mul, shift, w_zxbc, w_dt, conv_w, conv_b)


def _softplus(v):
    return jnp.maximum(v, 0.0) + jnp.log1p(jnp.exp(-jnp.abs(v)))


def _ssd_kernel(*refs, rev, epilogue):
    if epilogue:
        (x_ref, b_ref, c_ref, dt_ref, dtb_ref, an_ref, h0_ref, z_ref, yf_ref, dsk_ref, nw_ref,
         y_ref, hf_ref, h_scr) = refs
    else:
        x_ref, b_ref, c_ref, dt_ref, dtb_ref, an_ref, h0_ref, y_ref, hf_ref, h_scr = refs
    q = SSD_CHUNK
    hd = SSM_HEAD_DIM
    step = pl.program_id(0)

    @pl.when(step == 0)
    def _():
        h_scr[...] = h0_ref[...]

    dt = _softplus(dt_ref[...] + dtb_ref[...])
    d_a = dt * an_ref[...]
    row = lax.broadcasted_iota(jnp.int32, (q, q), 0)
    col = lax.broadcasted_iota(jnp.int32, (q, q), 1)
    tmask = (col >= row) if rev else (row >= col)
    cum = jnp.dot(tmask.astype(F32), d_a, preferred_element_type=F32, precision=HIGHEST)
    tot = cum[0:1, :] if rev else cum[q - 1:q, :]
    ecum = jnp.exp(cum)
    fin = dt * jnp.exp(tot - cum)
    etot = jnp.exp(tot)
    cum_t = cum.T
    dt_t = dt.T
    lo = _lane_lt((q, LANES), hd)
    lo1 = _lane_lt((1, LANES), hd)

    for g in range(SSM_GROUPS):
        bg = b_ref[:, g * SSM_STATE:(g + 1) * SSM_STATE]
        cg = c_ref[:, g * SSM_STATE:(g + 1) * SSM_STATE]
        cb = lax.dot_general(cg, bg, (((1,), (1,)), ((), ())), preferred_element_type=F32)
        h_t = h_scr[g]
        yoff = jnp.dot(cg, h_t.astype(BF16), preferred_element_type=F32)
        w_pairs = []
        dec_rows = []
        gated = []
        for pair in range(SSM_HPG // 2):
            ms = []
            for rr in range(2):
                h = g * SSM_HPG + 2 * pair + rr
                seg = cum[:, h:h + 1] - cum_t[h:h + 1, :]
                dec = jnp.exp(jnp.where(tmask, seg, -jnp.inf))
                ms.append((cb * dec * dt_t[h:h + 1, :]).astype(BF16))
            h0i = g * SSM_HPG + 2 * pair
            c0 = g * SSM_HPG * hd + pair * LANES
            xp = x_ref[:, c0:c0 + LANES]
            zero = jnp.zeros_like(xp)
            xcat = jnp.concatenate([jnp.where(lo, xp, zero), jnp.where(lo, zero, xp)], axis=0)
            ydiag = jnp.dot(jnp.concatenate(ms, axis=1), xcat, preferred_element_type=F32)
            esc = jnp.where(lo, ecum[:, h0i:h0i + 1], ecum[:, h0i + 1:h0i + 2])
            y_pair = ydiag + yoff[:, pair * LANES:(pair + 1) * LANES] * esc
            fsc = jnp.where(lo, fin[:, h0i:h0i + 1], fin[:, h0i + 1:h0i + 2])
            w_pairs.append((xp.astype(F32) * fsc).astype(BF16))
            dec_rows.append(jnp.where(lo1, etot[:, h0i:h0i + 1], etot[:, h0i + 1:h0i + 2]))
            if epilogue:
                yt = y_pair + yf_ref[:, c0:c0 + LANES].astype(F32) + dsk_ref[:, c0:c0 + LANES] * xp.astype(F32)
                gated.append(yt * _silu(z_ref[:, c0:c0 + LANES].astype(F32)))
            else:
                y_ref[:, c0:c0 + LANES] = y_pair.astype(y_ref.dtype)
        if epilogue:
            ssq = sum(jnp.sum(v * v, axis=-1, keepdims=True) for v in gated)
            rinv = lax.rsqrt(ssq * (1.0 / (SSM_HPG * hd)) + EPS)
            for pair, v in enumerate(gated):
                c0 = g * SSM_HPG * hd + pair * LANES
                y_ref[:, c0:c0 + LANES] = (v * rinv * nw_ref[:, c0:c0 + LANES]).astype(y_ref.dtype)
        wg = jnp.concatenate(w_pairs, axis=1)
        st = lax.dot_general(bg, wg, (((0,), (0,)), ((), ())), preferred_element_type=F32)
        h_scr[g] = h_t * jnp.concatenate(dec_rows, axis=1) + st

    @pl.when(step == pl.num_programs(0) - 1)
    def _():
        hf_ref[...] = h_scr[...]


def _ssd_scan(zxbc, dt_raw, dt_bias, a_neg, h0, rev, y_other=None, d_skip=None, norm_w=None):
    rows = zxbc.shape[0]
    q = SSD_CHUNK
    nc = rows // q
    epilogue = y_other is not None
    d = 1 if rev else 0

    def cidx(s):
        return (nc - 1 - s) if rev else s

    in_specs = [
        pl.BlockSpec((q, SSM_INNER), lambda s: (cidx(s), 1)),
        pl.BlockSpec((q, SSM_GN), lambda s: (cidx(s), 2 * SSM_INNER // SSM_GN)),
        pl.BlockSpec((q, SSM_GN), lambda s: (cidx(s), 2 * SSM_INNER // SSM_GN + 1)),
        pl.BlockSpec((q, LANES), lambda s: (cidx(s), d)),
        pl.BlockSpec((1, LANES), lambda s: (0, 0)),
        pl.BlockSpec((1, LANES), lambda s: (0, 0)),
        pl.BlockSpec((SSM_GROUPS, SSM_STATE, SSM_HPG * SSM_HEAD_DIM), lambda s: (0, 0, 0)),
    ]
    args = [zxbc, zxbc, zxbc, dt_raw, dt_bias, a_neg, h0]
    if epilogue:
        in_specs += [
            pl.BlockSpec((q, SSM_INNER), lambda s: (cidx(s), 0)),
            pl.BlockSpec((q, SSM_INNER), lambda s: (cidx(s), 0)),
            pl.BlockSpec((1, SSM_INNER), lambda s: (0, 0)),
            pl.BlockSpec((1, SSM_INNER), lambda s: (0, 0)),
        ]
        args += [zxbc, y_other, d_skip, norm_w]
    state_shape = (SSM_GROUPS, SSM_STATE, SSM_HPG * SSM_HEAD_DIM)
    return pl.pallas_call(
        functools.partial(_ssd_kernel, rev=rev, epilogue=epilogue),
        grid=(nc,),
        in_specs=in_specs,
        out_specs=[
            pl.BlockSpec((q, SSM_INNER), lambda s: (cidx(s), 0)),
            pl.BlockSpec(state_shape, lambda s: (0, 0, 0)),
        ],
        out_shape=[
            jax.ShapeDtypeStruct((rows, SSM_INNER), BF16),
            jax.ShapeDtypeStruct(state_shape, F32),
        ],
        scratch_shapes=[pltpu.VMEM(state_shape, F32)],
        compiler_params=_cparams(("arbitrary",)),
        name="ssd_bwd" if rev else "ssd_fwd",
    )(*args)


def _sgu_kernel(u_ref, v_ref, x_ref, lnw_ref, lnb_ref, ws_ref, bs_ref, wo_ref, g_ref, o_ref,
                vn_scr, uv_scr, *, tm):
    v = v_ref[...].astype(F32)
    mu = jnp.mean(v, axis=-1, keepdims=True)
    vc = v - mu
    var = jnp.mean(vc * vc, axis=-1, keepdims=True)
    vn_scr[...] = (vc * lax.rsqrt(var + EPS) * lnw_ref[...] + lnb_ref[...]).astype(BF16)
    gw = SGU_INNER // SGU_GROUPS
    for ch in range(tm // TOKEN_CHUNK):
        rs = slice(ch * TOKEN_CHUNK, (ch + 1) * TOKEN_CHUNK)
        for g in range(SGU_GROUPS):
            cs = slice(g * gw, (g + 1) * gw)
            sv = jnp.dot(ws_ref[g], vn_scr[rs, cs], preferred_element_type=F32) + bs_ref[:, cs]
            uv_scr[rs, cs] = (u_ref[rs, cs].astype(F32) * sv).astype(BF16)
    y = jnp.dot(uv_scr[...], wo_ref[...], preferred_element_type=F32)
    o_ref[...] = x_ref[...] + g_ref[...] * y


def _sgu_core(z, x, ln_w, ln_b, w_s, bs_exp, w_out, gate):
    rows, d = x.shape
    e = SGU_INNER
    tm = _row_tile(rows, 512)
    return pl.pallas_call(
        functools.partial(_sgu_kernel, tm=tm),
        grid=(rows // tm,),
        in_specs=[
            pl.BlockSpec((tm, e), lambda i: (i, 0)),
            pl.BlockSpec((tm, e), lambda i: (i, 1)),
            pl.BlockSpec((tm, d), lambda i: (i, 0)),
            pl.BlockSpec((1, e), lambda i: (0, 0)),
            pl.BlockSpec((1, e), lambda i: (0, 0)),
            pl.BlockSpec((SGU_GROUPS, TOKEN_CHUNK, TOKEN_CHUNK), lambda i: (0, 0, 0)),
            pl.BlockSpec((TOKEN_CHUNK, e), lambda i: (0, 0)),
            pl.BlockSpec((e, d), lambda i: (0, 0)),
            pl.BlockSpec((1, d), lambda i: (0, 0)),
        ],
        out_specs=pl.BlockSpec((tm, d), lambda i: (i, 0)),
        out_shape=jax.ShapeDtypeStruct((rows, d), F32),
        scratch_shapes=[pltpu.VMEM((tm, e), BF16), pltpu.VMEM((tm, e), BF16)],
        compiler_params=_cparams(("arbitrary",)),
        name="sgu_core",
    )(z, z, x, ln_w, ln_b, w_s, bs_exp, w_out, gate)


def _softmax_pv(s_list, v_list):
    m = s_list[0].max(axis=-1, keepdims=True)
    for s in s_list[1:]:
        m = jnp.maximum(m, s.max(axis=-1, keepdims=True))
    ps = [jnp.exp(s - m) for s in s_list]
    den = sum(p.sum(axis=-1, keepdims=True) for p in ps)
    o = sum(jnp.dot(p.astype(BF16), v, preferred_element_type=F32) for p, v in zip(ps, v_list))
    return o * (1.0 / den)


_NT = (((1,), (1,)), ((), ()))


def _na_kernel(q_ref, k_ref, v_ref, kc_ref, vc_ref, bias_ref, o_ref, *, rows, nblk):
    rb = pl.program_id(1)
    wstart = jnp.clip(rb * NA_RB - NA_ROW_WIN // 2, 0, rows - NA_WR)
    variant = jnp.where(rb == 0, 0, jnp.where(rb == nblk - 1, 2, 1))
    koff = pl.multiple_of(wstart * GRID_W, GRID_W)
    kw = k_ref[pl.ds(koff, NA_WR * GRID_W), :]
    vw = v_ref[pl.ds(koff, NA_WR * GRID_W), :]
    kc = kc_ref[...]
    vc = vc_ref[...]
    qv = q_ref[...]
    lo = _lane_lt(qv.shape, NA_HEAD_DIM)
    zero = jnp.zeros_like(qv)
    outs = []
    for hh in range(2):
        qm = jnp.where(lo, qv, zero) if hh == 0 else jnp.where(lo, zero, qv)
        s_win = lax.dot_general(qm, kw, _NT, preferred_element_type=F32) + bias_ref[hh, variant]
        s_ctx = lax.dot_general(qm, kc, _NT, preferred_element_type=F32)
        outs.append(_softmax_pv([s_win, s_ctx], [vw, vc]))
    o_ref[...] = jnp.where(lo, outs[0], outs[1]).astype(o_ref.dtype)


def _na_attention(qkv_lat, qkv_ctx, bias):
    n_lat = qkv_lat.shape[0]
    n_ctx = qkv_ctx.shape[0]
    rows = n_lat // GRID_W
    nblk = rows // NA_RB
    assert rows >= NA_WR and rows % NA_RB == 0
    hp_n = NA_HEADS // 2
    tq = NA_RB * GRID_W
    return pl.pallas_call(
        functools.partial(_na_kernel, rows=rows, nblk=nblk),
        grid=(hp_n, nblk),
        in_specs=[
            pl.BlockSpec((tq, LANES), lambda hp, rb: (rb, hp)),
            pl.BlockSpec((n_lat, LANES), lambda hp, rb: (0, hp_n + hp)),
            pl.BlockSpec((n_lat, LANES), lambda hp, rb: (0, 2 * hp_n + hp)),
            pl.BlockSpec((n_ctx, LANES), lambda hp, rb: (0, hp_n + hp)),
            pl.BlockSpec((n_ctx, LANES), lambda hp, rb: (0, 2 * hp_n + hp)),
            pl.BlockSpec((2, 3, tq, NA_WR * GRID_W), lambda hp, rb: (hp, 0, 0, 0)),
        ],
        out_specs=pl.BlockSpec((tq, LANES), lambda hp, rb: (rb, hp)),
        out_shape=jax.ShapeDtypeStruct((n_lat, D_MODEL), BF16),
        compiler_params=_cparams(("arbitrary", "arbitrary")),
        name="na_attention",
    )(qkv_lat, qkv_lat, qkv_lat, qkv_ctx, qkv_ctx, bias)


def _ctx_attn_kernel(q_ref, k_ref, v_ref, o_ref):
    qv = q_ref[...]
    lo = _lane_lt(qv.shape, NA_HEAD_DIM)
    zero = jnp.zeros_like(qv)
    outs = []
    for hh in range(2):
        qm = jnp.where(lo, qv, zero) if hh == 0 else jnp.where(lo, zero, qv)
        s = lax.dot_general(qm, k_ref[...], _NT, preferred_element_type=F32)
        outs.append(_softmax_pv([s], [v_ref[...]]))
    o_ref[...] = jnp.where(lo, outs[0], outs[1]).astype(o_ref.dtype)


def _ctx_attention(qkv_ctx):
    n_ctx = qkv_ctx.shape[0]
    hp_n = NA_HEADS // 2
    return pl.pallas_call(
        _ctx_attn_kernel,
        grid=(hp_n,),
        in_specs=[
            pl.BlockSpec((n_ctx, LANES), lambda hp: (0, hp)),
            pl.BlockSpec((n_ctx, LANES), lambda hp: (0, hp_n + hp)),
            pl.BlockSpec((n_ctx, LANES), lambda hp: (0, 2 * hp_n + hp)),
        ],
        out_specs=pl.BlockSpec((n_ctx, LANES), lambda hp: (0, hp)),
        out_shape=jax.ShapeDtypeStruct((n_ctx, D_MODEL), BF16),
        compiler_params=_cparams(("arbitrary",)),
        name="ctx_attention",
    )(qkv_ctx, qkv_ctx, qkv_ctx)


def _na_bias_table(rpb, rows):
    col = np.arange(GRID_W)
    col_start = np.clip(col - NA_COL_WIN // 2, 0, GRID_W - NA_COL_WIN)
    in_win = (col[None, :] >= col_start[:, None]) & (col[None, :] < col_start[:, None] + NA_COL_WIN)
    col_idx = np.clip(col[None, :] - col[:, None] + NA_COL_WIN - 1, 0, 2 * NA_COL_WIN - 2)
    col_bias = jnp.where(in_win, rpb.astype(F32)[:, :, col_idx], -jnp.inf)
    wr = min(NA_ROW_WIN, rows)
    idx = np.zeros((3, NA_RB, NA_WR), np.int32)
    valid = np.zeros((3, NA_RB, NA_WR), bool)
    for var, r0 in enumerate((0, NA_RB, rows - NA_RB)):
        wstart = int(np.clip(r0 - NA_ROW_WIN // 2, 0, rows - NA_WR))
        for qr in range(NA_RB):
            r = r0 + qr
            rs = int(np.clip(r - wr // 2, 0, rows - wr))
            for kj in range(NA_WR):
                kr = wstart + kj
                if rs <= kr < rs + wr:
                    valid[var, qr, kj] = True
                    idx[var, qr, kj] = kr - r + NA_ROW_WIN - 1
    t = col_bias[:, idx]
    t = jnp.where(valid[None, :, :, :, None, None], t, -jnp.inf)
    t = jnp.transpose(t, (0, 1, 2, 4, 3, 5))
    return t.reshape(NA_HEADS, 3, NA_RB * GRID_W, NA_WR * GRID_W)


def _ffn_kernel(xp_ref, x_ref, xn_ref, mul_ref, sh_ref, gate_ref, wup_ref, cw_ref, cb_ref, wdn_ref,
                o_ref, h_scr, acc_scr, act_scr, *, tm, nblk):
    _fill_h_with_halo(h_scr, xp_ref, x_ref, xn_ref, mul_ref[...], sh_ref[...], tm, nblk)
    hc = FFN_CHUNK

    def chunk(c, carry):
        acc_scr[...] = jnp.dot(h_scr[...], wup_ref[c], preferred_element_type=F32)
        y = _dwconv_from_scratch(acc_scr, cw_ref.at[c], cb_ref.at[c], FFN_CONV, tm)
        act_scr[c] = (_silu(y[:, :hc]) * y[:, hc:]).astype(BF16)
        return carry

    lax.fori_loop(0, FFN_HIDDEN // hc, chunk, 0)
    y = jnp.dot(act_scr[0], wdn_ref[0], preferred_element_type=F32)
    for c in range(1, FFN_HIDDEN // hc):
        y = y + jnp.dot(act_scr[c], wdn_ref[c], preferred_element_type=F32)
    o_ref[...] = x_ref[...] + gate_ref[...] * y


def _ffn(x, mul, shift, gate, wup_c, cw_c, cb_c, wdn_c):
    rows, d = x.shape
    nch, _, hc2 = wup_c.shape
    tm = _row_tile(rows, 512)
    nblk = rows // tm
    in_specs = _halo_specs(tm, d, rows) + [
        pl.BlockSpec((1, d), lambda i: (0, 0)),
        pl.BlockSpec((1, d), lambda i: (0, 0)),
        pl.BlockSpec((1, d), lambda i: (0, 0)),
        pl.BlockSpec((nch, d, hc2), lambda i: (0, 0, 0)),
        pl.BlockSpec((nch, FFN_CONV, hc2), lambda i: (0, 0, 0)),
        pl.BlockSpec((nch, 1, hc2), lambda i: (0, 0, 0)),
        pl.BlockSpec((nch, hc2 // 2, d), lambda i: (0, 0, 0)),
    ]
    return pl.pallas_call(
        functools.partial(_ffn_kernel, tm=tm, nblk=nblk),
        grid=(nblk,),
        in_specs=in_specs,
        out_specs=pl.BlockSpec((tm, d), lambda i: (i, 0)),
        out_shape=jax.ShapeDtypeStruct((rows, d), F32),
        scratch_shapes=[
            pltpu.VMEM((tm + 2 * HALO, d), BF16),
            pltpu.VMEM((tm + 2 * HALO, hc2), F32),
            pltpu.VMEM((nch, tm, hc2 // 2), BF16),
        ],
        compiler_params=_cparams(("arbitrary",)),
        name="conv_ffn",
    )(x, x, x, mul, shift, gate, wup_c, cw_c, cb_c, wdn_c)


def _ffn_chunked(w):
    lead = w.shape[:-1]
    nch = FFN_HIDDEN // FFN_CHUNK
    w2 = w.reshape(lead + (2, nch, FFN_CHUNK))
    w2 = jnp.moveaxis(w2, -2, 0)
    return w2.reshape((nch,) + lead + (2 * FFN_CHUNK,))


def _pad_lanes(v, n):
    return jnp.pad(v, [(0, 0)] * (v.ndim - 1) + [(0, n - v.shape[-1])])


def kernel(x, c, ctx, c_ctx, norm_w, w_mod, b_mod, ssm_w_in, ssm_conv_w, ssm_conv_b, ssm_a_log, ssm_dt_bias, ssm_d_skip, ssm_norm_w, ssm_w_out, sgu_w_in, sgu_ln_w, sgu_ln_b, sgu_w_s, sgu_b_s, sgu_w_out, na_w_qkv, na_q_norm, na_k_norm, na_rpb, na_w_out, ffn_w_up, ffn_conv_w, ffn_conv_b, ffn_w_down):
    assert x.shape[0] == 1 and c.shape[0] == 1
    d = D_MODEL
    depth = w_mod.shape[0]
    x_lat = x[0]
    x_ctx = ctx[0]
    cond = jnp.zeros((8, d), F32).at[0].set(c[0]).at[1].set(c_ctx)
    mods = _mod_vectors(cond, w_mod, b_mod)

    def row(v):
        return v.reshape(1, -1)

    for i in range(depth):
        kind, j = i % N_MIXERS, i // N_MIXERS
        need_ctx = i < depth - 1
        ml = [row(mods[i, 0, k * d:(k + 1) * d]) for k in range(6)]
        mc = [row(mods[i, 1, k * d:(k + 1) * d]) for k in range(6)]
        nw0, nw1 = row(norm_w[i, 0]), row(norm_w[i, 1])
        mul_l, mul_c = nw0 * (1.0 + ml[1]), nw0 * (1.0 + mc[1])

        if kind == 0:
            w_in = ssm_w_in[j]
            w_zxbc = w_in[:, :SSM_ZXBC].astype(BF16)
            w_dt = w_in[:, SSM_ZXBC:]
            w_dt = jnp.concatenate([_pad_lanes(w_dt[:, :SSM_HEADS], LANES),
                                    _pad_lanes(w_dt[:, SSM_HEADS:], LANES)], axis=1).astype(BF16)
            cw = jnp.concatenate([jnp.zeros((SSM_CONV, SSM_INNER), F32), ssm_conv_w[j]], axis=1)
            cb = jnp.concatenate([jnp.zeros((SSM_INNER,), F32), ssm_conv_b[j]]).reshape(1, -1)
            a_neg = _pad_lanes(-jnp.exp(ssm_a_log[j]), LANES)
            dtb = _pad_lanes(ssm_dt_bias[j], LANES)
            dsk = row(jnp.repeat(ssm_d_skip[j, 0] + ssm_d_skip[j, 1], SSM_HEAD_DIM))
            gnw = row(ssm_norm_w[j])
            w_out = ssm_w_out[j].astype(BF16)
            h0 = jnp.zeros((SSM_GROUPS, SSM_STATE, SSM_HPG * SSM_HEAD_DIM), F32)

            def mixer(xs, mul, shift, hf0, hb0):
                zxbc, dt_raw = _ssm_in(xs, mul, shift, w_zxbc, w_dt, cw, cb)
                yf, hf = _ssd_scan(zxbc, dt_raw, dtb[0:1], a_neg[0:1], hf0, False)
                gn, hb = _ssd_scan(zxbc, dt_raw, dtb[1:2], a_neg[1:2], hb0, True,
                                   y_other=yf, d_skip=dsk, norm_w=gnw)
                return gn, hf, hb

            gn_c, hf, hb = mixer(x_ctx, mul_c, mc[0], h0, h0)
            gn_l, _, _ = mixer(x_lat, mul_l, ml[0], hf, hb)
            x_lat = _mm_res(gn_l, w_out, x_lat, ml[2])
            if need_ctx:
                x_ctx = _mm_res(gn_c, w_out, x_ctx, mc[2])
        elif kind == 1:
            w_in = sgu_w_in[j].astype(BF16)
            w_s = sgu_w_s[j].astype(BF16)
            gw = SGU_INNER // SGU_GROUPS
            bs_exp = jnp.repeat(sgu_b_s[j].T, gw, axis=1)
            w_out = sgu_w_out[j].astype(BF16)
            lnw, lnb = row(sgu_ln_w[j]), row(sgu_ln_b[j])
            z_l = _modmm(x_lat, mul_l, ml[0], w_in, "gelu")
            x_lat = _sgu_core(z_l, x_lat, lnw, lnb, w_s, bs_exp, w_out, ml[2])
            if need_ctx:
                z_c = _modmm(x_ctx, mul_c, mc[0], w_in, "gelu")
                x_ctx = _sgu_core(z_c, x_ctx, lnw, lnb, w_s, bs_exp, w_out, mc[2])
        else:
            w_qkv = na_w_qkv[j].astype(BF16)
            scale = NA_HEAD_DIM ** -0.5
            nw = jnp.concatenate([jnp.tile(na_q_norm[j] * scale, NA_HEADS),
                                  jnp.tile(na_k_norm[j], NA_HEADS),
                                  jnp.ones((d,), F32)]).reshape(1, -1)
            w_out = na_w_out[j].astype(BF16)
            bias = _na_bias_table(na_rpb[j], x_lat.shape[0] // GRID_W)
            qkv_c = _modmm(x_ctx, mul_c, mc[0], w_qkv, "qkv", nw)
            qkv_l = _modmm(x_lat, mul_l, ml[0], w_qkv, "qkv", nw)
            o_l = _na_attention(qkv_l, qkv_c, bias)
            x_lat = _mm_res(o_l, w_out, x_lat, ml[2])
            if need_ctx:
                o_c = _ctx_attention(qkv_c)
                x_ctx = _mm_res(o_c, w_out, x_ctx, mc[2])

        nch = FFN_HIDDEN // FFN_CHUNK
        wup_c = _ffn_chunked(ffn_w_up[i]).astype(BF16)
        cw_c = _ffn_chunked(ffn_conv_w[i])
        cb_c = _ffn_chunked(ffn_conv_b[i].reshape(1, -1))
        wdn_c = ffn_w_down[i].astype(BF16).reshape(nch, FFN_CHUNK, d)
        x_lat = _ffn(x_lat, nw1 * (1.0 + ml[4]), ml[3], ml[5], wup_c, cw_c, cb_c, wdn_c)
        if need_ctx:
            x_ctx = _ffn(x_ctx, nw1 * (1.0 + mc[4]), mc[3], mc[5], wup_c, cw_c, cb_c, wdn_c)
    return x_lat[None]
```

```python
import functools
import math

import numpy as np
import jax
import jax.numpy as jnp
from jax import lax
from jax.experimental import pallas as pl
from jax.experimental.pallas import tpu as pltpu

F32 = jnp.float32
BF16 = jnp.bfloat16
HIGHEST = lax.Precision.HIGHEST

D_MODEL = 1024
DEPTH = 4
N_MIXERS = 3
EPS = 1e-6
GRID_W = 64
SSM_INNER = 2 * D_MODEL
SSM_HEAD_DIM = 64
SSM_HEADS = SSM_INNER // SSM_HEAD_DIM
SSM_GROUPS = 8
SSM_HPG = SSM_HEADS // SSM_GROUPS
SSM_STATE = 128
SSM_CONV = 7
SSD_CHUNK = 128
SSM_GN = SSM_GROUPS * SSM_STATE
SSM_ZXBC = 2 * SSM_INNER + 2 * SSM_GN
SGU_INNER = 2 * D_MODEL
SGU_GROUPS = 8
TOKEN_CHUNK = 128
NA_HEAD_DIM = 64
NA_HEADS = D_MODEL // NA_HEAD_DIM
NA_ROW_WIN = 8
NA_COL_WIN = 16
FFN_HIDDEN = 2816
FFN_CONV = 3

LANES = 128
BF16_SUBLANES = 16
VMEM_LIMIT = 56 * 1024 * 1024

HALO = BF16_SUBLANES
FFN_CHUNK = 256
NA_RB = 4
NA_WR = NA_RB + NA_ROW_WIN


def _cparams(sem, flags=None):
    return pltpu.CompilerParams(dimension_semantics=sem, vmem_limit_bytes=VMEM_LIMIT, flags=flags)


def _row_tile(n, pref):
    t = min(n, pref)
    assert n % t == 0
    return t


def _sigmoid(v):
    return 1.0 / (1.0 + jnp.exp(-v))


def _silu(v):
    return v * _sigmoid(v)


def _modulate(x, mul, shift):
    ms = jnp.mean(x * x, axis=-1, keepdims=True)
    return x * lax.rsqrt(ms + EPS) * mul + shift


def _lane_lt(shape, n):
    return lax.broadcasted_iota(jnp.int32, shape, len(shape) - 1) < n


def _mod_kernel(c_ref, w_ref, b_ref, o_ref):
    s = _silu(c_ref[...])
    o_ref[0] = jnp.dot(s, w_ref[0], preferred_element_type=F32, precision=HIGHEST) + b_ref[0]


def _mod_vectors(cond, w_mod, b_mod):
    depth, d, n = w_mod.shape
    tn = 1536
    return pl.pallas_call(
        _mod_kernel,
        grid=(depth, n // tn),
        in_specs=[
            pl.BlockSpec((8, d), lambda i, j: (0, 0)),
            pl.BlockSpec((1, d, tn), lambda i, j: (i, 0, j)),
            pl.BlockSpec((1, 1, tn), lambda i, j: (i, 0, j)),
        ],
        out_specs=pl.BlockSpec((1, 8, tn), lambda i, j: (i, 0, j)),
        out_shape=jax.ShapeDtypeStruct((depth, 8, n), F32),
        compiler_params=_cparams(("arbitrary", "arbitrary")),
        name="mod_vectors",
    )(cond, w_mod, b_mod.reshape(depth, 1, n))


def _gelu_tanh(v):
    c = math.sqrt(2.0 / math.pi)
    return v * (0.5 * (1.0 + jnp.tanh(c * (v + 0.044715 * (v * v * v)))))


def _head_rmsnorm(blk, nw):
    lo = _lane_lt(blk.shape, NA_HEAD_DIM)
    sq = blk * blk
    s_lo = jnp.sum(jnp.where(lo, sq, 0.0), axis=-1, keepdims=True)
    s_hi = jnp.sum(jnp.where(lo, 0.0, sq), axis=-1, keepdims=True)
    r_lo = lax.rsqrt(s_lo * (1.0 / NA_HEAD_DIM) + EPS)
    r_hi = lax.rsqrt(s_hi * (1.0 / NA_HEAD_DIM) + EPS)
    return blk * jnp.where(lo, r_lo, r_hi) * nw


def _modmm_kernel(x_ref, mul_ref, sh_ref, w_ref, *rest, mode, tn):
    if mode == "qkv":
        nw_ref, o_ref, h_scr = rest
    else:
        o_ref, h_scr = rest
    j = pl.program_id(1)

    @pl.when(j == 0)
    def _():
        h_scr[...] = _modulate(x_ref[...], mul_ref[...], sh_ref[...]).astype(BF16)

    if mode == "gelu":
        acc = jnp.dot(h_scr[...], w_ref[...], preferred_element_type=F32)
        o_ref[...] = _gelu_tanh(acc).astype(o_ref.dtype)
    else:
        n_norm = 2 * D_MODEL // tn

        @pl.when(j < n_norm)
        def _():
            acc = jnp.dot(h_scr[...], w_ref[...], preferred_element_type=F32)
            for b in range(tn // LANES):
                sl = slice(b * LANES, (b + 1) * LANES)
                o_ref[:, sl] = _head_rmsnorm(acc[:, sl], nw_ref[:, sl]).astype(o_ref.dtype)

        @pl.when(j >= n_norm)
        def _():
            acc = jnp.dot(h_scr[...], w_ref[...], preferred_element_type=F32)
            o_ref[...] = acc.astype(o_ref.dtype)


def _modmm(x, mul, shift, w, mode, nw=None):
    rows, d = x.shape
    n = w.shape[1]
    tm = _row_tile(rows, 1024)
    tn = 512
    in_specs = [
        pl.BlockSpec((tm, d), lambda i, j: (i, 0)),
        pl.BlockSpec((1, d), lambda i, j: (0, 0)),
        pl.BlockSpec((1, d), lambda i, j: (0, 0)),
        pl.BlockSpec((d, tn), lambda i, j: (0, j)),
    ]
    args = [x, mul, shift, w]
    if mode == "qkv":
        in_specs.append(pl.BlockSpec((1, tn), lambda i, j: (0, j)))
        args.append(nw)
    return pl.pallas_call(
        functools.partial(_modmm_kernel, mode=mode, tn=tn),
        grid=(rows // tm, n // tn),
        in_specs=in_specs,
        out_specs=pl.BlockSpec((tm, tn), lambda i, j: (i, j)),
        out_shape=jax.ShapeDtypeStruct((rows, n), BF16),
        scratch_shapes=[pltpu.VMEM((tm, d), BF16)],
        compiler_params=_cparams(("arbitrary", "arbitrary")),
        name="modmm_" + mode,
    )(*args)


def _mmres_kernel(a_ref, w_ref, x_ref, g_ref, o_ref):
    y = jnp.dot(a_ref[...], w_ref[...], preferred_element_type=F32)
    o_ref[...] = x_ref[...] + g_ref[...] * y


def _mm_res(a, w, x, gate):
    rows, k = a.shape
    d = w.shape[1]
    tm = _row_tile(rows, 1024)
    return pl.pallas_call(
        _mmres_kernel,
        grid=(rows // tm,),
        in_specs=[
            pl.BlockSpec((tm, k), lambda i: (i, 0)),
            pl.BlockSpec((k, d), lambda i: (0, 0)),
            pl.BlockSpec((tm, d), lambda i: (i, 0)),
            pl.BlockSpec((1, d), lambda i: (0, 0)),
        ],
        out_specs=pl.BlockSpec((tm, d), lambda i: (i, 0)),
        out_shape=jax.ShapeDtypeStruct((rows, d), F32),
        compiler_params=_cparams(("arbitrary",)),
        name="mm_res",
    )(a, w, x, gate)


def _fill_h_with_halo(h_scr, xp_ref, x_ref, xn_ref, mul, sh, tm, nblk):
    i = pl.program_id(0)
    h_scr[HALO:HALO + tm, :] = _modulate(x_ref[...], mul, sh).astype(BF16)
    hp = jnp.where(i > 0, _modulate(xp_ref[...], mul, sh), 0.0)
    hn = jnp.where(i < nblk - 1, _modulate(xn_ref[...], mul, sh), 0.0)
    h_scr[0:HALO, :] = hp.astype(BF16)
    h_scr[HALO + tm:HALO + tm + HALO, :] = hn.astype(BF16)


def _halo_specs(tm, d, rows):
    per = tm // HALO
    last = rows // HALO - 1
    return [
        pl.BlockSpec((HALO, d), lambda i, *_: (jnp.maximum(i * per - 1, 0), 0)),
        pl.BlockSpec((tm, d), lambda i, *_: (i, 0)),
        pl.BlockSpec((HALO, d), lambda i, *_: (jnp.minimum((i + 1) * per, last), 0)),
    ]


def _dwconv_from_scratch(acc_scr, cw_ref, cb_ref, taps, tm):
    base = HALO - taps // 2
    y = cb_ref[...] + cw_ref[0:1, :] * acc_scr[pl.ds(base, tm), :]
    for k in range(1, taps):
        y = y + cw_ref[k:k + 1, :] * acc_scr[pl.ds(base + k, tm), :]
    return y


def _ssm_in_kernel(xp_ref, x_ref, xn_ref, mul_ref, sh_ref, w_ref, wdt_ref, cw_ref, cb_ref,
                   o_ref, dt_ref, h_scr, acc_scr, *, tm, nblk, n_plain):
    j = pl.program_id(1)

    @pl.when(j == 0)
    def _():
        _fill_h_with_halo(h_scr, xp_ref, x_ref, xn_ref, mul_ref[...], sh_ref[...], tm, nblk)
        dt_ref[...] = jnp.dot(h_scr[HALO:HALO + tm, :], wdt_ref[...], preferred_element_type=F32)

    @pl.when(j < n_plain)
    def _():
        acc = jnp.dot(h_scr[HALO:HALO + tm, :], w_ref[...], preferred_element_type=F32)
        o_ref[...] = acc.astype(o_ref.dtype)

    @pl.when(j >= n_plain)
    def _():
        acc_scr[...] = jnp.dot(h_scr[...], w_ref[...], preferred_element_type=F32)
        y = _dwconv_from_scratch(acc_scr, cw_ref, cb_ref, SSM_CONV, tm)
        o_ref[...] = _silu(y).astype(o_ref.dtype)


def _ssm_in(x, mul, shift, w_zxbc, w_dt, conv_w, conv_b):
    rows, d = x.shape
    n = w_zxbc.shape[1]
    tm = _row_tile(rows, 1024)
    tn = 512
    nblk = rows // tm
    ndt = w_dt.shape[1]
    in_specs = _halo_specs(tm, d, rows) + [
        pl.BlockSpec((1, d), lambda i, j: (0, 0)),
        pl.BlockSpec((1, d), lambda i, j: (0, 0)),
        pl.BlockSpec((d, tn), lambda i, j: (0, j)),
        pl.BlockSpec((d, ndt), lambda i, j: (0, 0)),
        pl.BlockSpec((SSM_CONV, tn), lambda i, j: (0, j)),
        pl.BlockSpec((1, tn), lambda i, j: (0, j)),
    ]
    return pl.pallas_call(
        functools.partial(_ssm_in_kernel, tm=tm, nblk=nblk, n_plain=SSM_INNER // tn),
        grid=(nblk, n // tn),
        in_specs=in_specs,
        out_specs=[
            pl.BlockSpec((tm, tn), lambda i, j: (i, j)),
            pl.BlockSpec((tm, ndt), lambda i, j: (i, 0)),
        ],
        out_shape=[
            jax.ShapeDtypeStruct((rows, n), BF16),
            jax.ShapeDtypeStruct((rows, ndt), F32),
        ],
        scratch_shapes=[
            pltpu.VMEM((tm + 2 * HALO, d), BF16),
            pltpu.VMEM((tm + 2 * HALO, tn), F32),
        ],
        compiler_params=_cparams(("arbitrary", "arbitrary")),
        name="ssm_in",
    )(x, x, x, mul, shift, w_zxbc, w_dt, conv_w, conv_b)


def _softplus(v):
    return jnp.maximum(v, 0.0) + jnp.log1p(jnp.exp(-jnp.abs(v)))


def _ssd_kernel(*refs, rev, epilogue):
    if epilogue:
        (x_ref, b_ref, c_ref, dt_ref, dtb_ref, an_ref, h0_ref, z_ref, yf_ref, dsk_ref, nw_ref,
         y_ref, hf_ref, h_scr) = refs
    else:
        x_ref, b_ref, c_ref, dt_ref, dtb_ref, an_ref, h0_ref, y_ref, hf_ref, h_scr = refs
    q = SSD_CHUNK
    hd = SSM_HEAD_DIM
    step = pl.program_id(0)

    @pl.when(step == 0)
    def _():
        h_scr[...] = h0_ref[...]

    dt = _softplus(dt_ref[...] + dtb_ref[...])
    d_a = dt * an_ref[...]
    row = lax.broadcasted_iota(jnp.int32, (q, q), 0)
    col = lax.broadcasted_iota(jnp.int32, (q, q), 1)
    tmask = (col >= row) if rev else (row >= col)
    cum = jnp.dot(tmask.astype(F32), d_a, preferred_element_type=F32, precision=HIGHEST)
    tot = cum[0:1, :] if rev else cum[q - 1:q, :]
    ecum = jnp.exp(cum)
    fin = dt * jnp.exp(tot - cum)
    etot = jnp.exp(tot)
    cum_t = cum.T
    dt_t = dt.T
    lo = _lane_lt((q, LANES), hd)
    lo1 = _lane_lt((1, LANES), hd)

    for g in range(SSM_GROUPS):
        bg = b_ref[:, g * SSM_STATE:(g + 1) * SSM_STATE]
        cg = c_ref[:, g * SSM_STATE:(g + 1) * SSM_STATE]
        cb = lax.dot_general(cg, bg, (((1,), (1,)), ((), ())), preferred_element_type=F32)
        h_t = h_scr[g]
        yoff = jnp.dot(cg, h_t.astype(BF16), preferred_element_type=F32)
        w_pairs = []
        dec_rows = []
        gated = []
        for pair in range(SSM_HPG // 2):
            ms = []
            for rr in range(2):
                h = g * SSM_HPG + 2 * pair + rr
                seg = cum[:, h:h + 1] - cum_t[h:h + 1, :]
                dec = jnp.exp(jnp.where(tmask, seg, -jnp.inf))
                ms.append((cb * dec * dt_t[h:h + 1, :]).astype(BF16))
            h0i = g * SSM_HPG + 2 * pair
            c0 = g * SSM_HPG * hd + pair * LANES
            xp = x_ref[:, c0:c0 + LANES]
            zero = jnp.zeros_like(xp)
            xcat = jnp.concatenate([jnp.where(lo, xp, zero), jnp.where(lo, zero, xp)], axis=0)
            ydiag = jnp.dot(jnp.concatenate(ms, axis=1), xcat, preferred_element_type=F32)
            esc = jnp.where(lo, ecum[:, h0i:h0i + 1], ecum[:, h0i + 1:h0i + 2])
            y_pair = ydiag + yoff[:, pair * LANES:(pair + 1) * LANES] * esc
            fsc = jnp.where(lo, fin[:, h0i:h0i + 1], fin[:, h0i + 1:h0i + 2])
            w_pairs.append((xp.astype(F32) * fsc).astype(BF16))
            dec_rows.append(jnp.where(lo1, etot[:, h0i:h0i + 1], etot[:, h0i + 1:h0i + 2]))
            if epilogue:
                yt = y_pair + yf_ref[:, c0:c0 + LANES].astype(F32) + dsk_ref[:, c0:c0 + LANES] * xp.astype(F32)
                gated.append(yt * _silu(z_ref[:, c0:c0 + LANES].astype(F32)))
            else:
                y_ref[:, c0:c0 + LANES] = y_pair.astype(y_ref.dtype)
        if epilogue:
            ssq = sum(jnp.sum(v * v, axis=-1, keepdims=True) for v in gated)
            rinv = lax.rsqrt(ssq * (1.0 / (SSM_HPG * hd)) + EPS)
            for pair, v in enumerate(gated):
                c0 = g * SSM_HPG * hd + pair * LANES
                y_ref[:, c0:c0 + LANES] = (v * rinv * nw_ref[:, c0:c0 + LANES]).astype(y_ref.dtype)
        wg = jnp.concatenate(w_pairs, axis=1)
        st = lax.dot_general(bg, wg, (((0,), (0,)), ((), ())), preferred_element_type=F32)
        h_scr[g] = h_t * jnp.concatenate(dec_rows, axis=1) + st

    @pl.when(step == pl.num_programs(0) - 1)
    def _():
        hf_ref[...] = h_scr[...]


def _ssd_scan(zxbc, dt_raw, dt_bias, a_neg, h0, rev, y_other=None, d_skip=None, norm_w=None):
    rows = zxbc.shape[0]
    q = SSD_CHUNK
    nc = rows // q
    epilogue = y_other is not None
    d = 1 if rev else 0

    def cidx(s):
        return (nc - 1 - s) if rev else s

    in_specs = [
        pl.BlockSpec((q, SSM_INNER), lambda s: (cidx(s), 1)),
        pl.BlockSpec((q, SSM_GN), lambda s: (cidx(s), 2 * SSM_INNER // SSM_GN)),
        pl.BlockSpec((q, SSM_GN), lambda s: (cidx(s), 2 * SSM_INNER // SSM_GN + 1)),
        pl.BlockSpec((q, LANES), lambda s: (cidx(s), d)),
        pl.BlockSpec((1, LANES), lambda s: (0, 0)),
        pl.BlockSpec((1, LANES), lambda s: (0, 0)),
        pl.BlockSpec((SSM_GROUPS, SSM_STATE, SSM_HPG * SSM_HEAD_DIM), lambda s: (0, 0, 0)),
    ]
    args = [zxbc, zxbc, zxbc, dt_raw, dt_bias, a_neg, h0]
    if epilogue:
        in_specs += [
            pl.BlockSpec((q, SSM_INNER), lambda s: (cidx(s), 0)),
            pl.BlockSpec((q, SSM_INNER), lambda s: (cidx(s), 0)),
            pl.BlockSpec((1, SSM_INNER), lambda s: (0, 0)),
            pl.BlockSpec((1, SSM_INNER), lambda s: (0, 0)),
        ]
        args += [zxbc, y_other, d_skip, norm_w]
    state_shape = (SSM_GROUPS, SSM_STATE, SSM_HPG * SSM_HEAD_DIM)
    return pl.pallas_call(
        functools.partial(_ssd_kernel, rev=rev, epilogue=epilogue),
        grid=(nc,),
        in_specs=in_specs,
        out_specs=[
            pl.BlockSpec((q, SSM_INNER), lambda s: (cidx(s), 0)),
            pl.BlockSpec(state_shape, lambda s: (0, 0, 0)),
        ],
        out_shape=[
            jax.ShapeDtypeStruct((rows, SSM_INNER), BF16),
            jax.ShapeDtypeStruct(state_shape, F32),
        ],
        scratch_shapes=[pltpu.VMEM(state_shape, F32)],
        compiler_params=_cparams(("arbitrary",)),
        name="ssd_bwd" if rev else "ssd_fwd",
    )(*args)


def _sgu_kernel(u_ref, v_ref, x_ref, lnw_ref, lnb_ref, ws_ref, bs_ref, wo_ref, g_ref, o_ref,
                vn_scr, uv_scr, *, tm):
    v = v_ref[...].astype(F32)
    mu = jnp.mean(v, axis=-1, keepdims=True)
    vc = v - mu
    var = jnp.mean(vc * vc, axis=-1, keepdims=True)
    vn_scr[...] = (vc * lax.rsqrt(var + EPS) * lnw_ref[...] + lnb_ref[...]).astype(BF16)
    gw = SGU_INNER // SGU_GROUPS
    for ch in range(tm // TOKEN_CHUNK):
        rs = slice(ch * TOKEN_CHUNK, (ch + 1) * TOKEN_CHUNK)
        for g in range(SGU_GROUPS):
            cs = slice(g * gw, (g + 1) * gw)
            sv = jnp.dot(ws_ref[g], vn_scr[rs, cs], preferred_element_type=F32) + bs_ref[:, cs]
            uv_scr[rs, cs] = (u_ref[rs, cs].astype(F32) * sv).astype(BF16)
    y = jnp.dot(uv_scr[...], wo_ref[...], preferred_element_type=F32)
    o_ref[...] = x_ref[...] + g_ref[...] * y


def _sgu_core(z, x, ln_w, ln_b, w_s, bs_exp, w_out, gate):
    rows, d = x.shape
    e = SGU_INNER
    tm = _row_tile(rows, 512)
    return pl.pallas_call(
        functools.partial(_sgu_kernel, tm=tm),
        grid=(rows // tm,),
        in_specs=[
            pl.BlockSpec((tm, e), lambda i: (i, 0)),
            pl.BlockSpec((tm, e), lambda i: (i, 1)),
            pl.BlockSpec((tm, d), lambda i: (i, 0)),
            pl.BlockSpec((1, e), lambda i: (0, 0)),
            pl.BlockSpec((1, e), lambda i: (0, 0)),
            pl.BlockSpec((SGU_GROUPS, TOKEN_CHUNK, TOKEN_CHUNK), lambda i: (0, 0, 0)),
            pl.BlockSpec((TOKEN_CHUNK, e), lambda i: (0, 0)),
            pl.BlockSpec((e, d), lambda i: (0, 0)),
            pl.BlockSpec((1, d), lambda i: (0, 0)),
        ],
        out_specs=pl.BlockSpec((tm, d), lambda i: (i, 0)),
        out_shape=jax.ShapeDtypeStruct((rows, d), F32),
        scratch_shapes=[pltpu.VMEM((tm, e), BF16), pltpu.VMEM((tm, e), BF16)],
        compiler_params=_cparams(("arbitrary",)),
        name="sgu_core",
    )(z, z, x, ln_w, ln_b, w_s, bs_exp, w_out, gate)


def _softmax_pv(s_list, v_list):
    m = s_list[0].max(axis=-1, keepdims=True)
    for s in s_list[1:]:
        m = jnp.maximum(m, s.max(axis=-1, keepdims=True))
    ps = [jnp.exp(s - m) for s in s_list]
    den = sum(p.sum(axis=-1, keepdims=True) for p in ps)
    o = sum(jnp.dot(p.astype(BF16), v, preferred_element_type=F32) for p, v in zip(ps, v_list))
    return o * (1.0 / den)


_NT = (((1,), (1,)), ((), ()))


def _na_kernel(q_ref, k_ref, v_ref, kc_ref, vc_ref, bias_ref, o_ref, *, rows, nblk):
    rb = pl.program_id(1)
    wstart = jnp.clip(rb * NA_RB - NA_ROW_WIN // 2, 0, rows - NA_WR)
    variant = jnp.where(rb == 0, 0, jnp.where(rb == nblk - 1, 2, 1))
    koff = pl.multiple_of(wstart * GRID_W, GRID_W)
    kw = k_ref[pl.ds(koff, NA_WR * GRID_W), :]
    vw = v_ref[pl.ds(koff, NA_WR * GRID_W), :]
    kc = kc_ref[...]
    vc = vc_ref[...]
    qv = q_ref[...]
    lo = _lane_lt(qv.shape, NA_HEAD_DIM)
    zero = jnp.zeros_like(qv)
    outs = []
    for hh in range(2):
        qm = jnp.where(lo, qv, zero) if hh == 0 else jnp.where(lo, zero, qv)
        s_win = lax.dot_general(qm, kw, _NT, preferred_element_type=F32) + bias_ref[hh, variant]
        s_ctx = lax.dot_general(qm, kc, _NT, preferred_element_type=F32)
        outs.append(_softmax_pv([s_win, s_ctx], [vw, vc]))
    o_ref[...] = jnp.where(lo, outs[0], outs[1]).astype(o_ref.dtype)


def _na_attention(qkv_lat, qkv_ctx, bias):
    n_lat = qkv_lat.shape[0]
    n_ctx = qkv_ctx.shape[0]
    rows = n_lat // GRID_W
    nblk = rows // NA_RB
    assert rows >= NA_WR and rows % NA_RB == 0
    hp_n = NA_HEADS // 2
    tq = NA_RB * GRID_W
    return pl.pallas_call(
        functools.partial(_na_kernel, rows=rows, nblk=nblk),
        grid=(hp_n, nblk),
        in_specs=[
            pl.BlockSpec((tq, LANES), lambda hp, rb: (rb, hp)),
            pl.BlockSpec((n_lat, LANES), lambda hp, rb: (0, hp_n + hp)),
            pl.BlockSpec((n_lat, LANES), lambda hp, rb: (0, 2 * hp_n + hp)),
            pl.BlockSpec((n_ctx, LANES), lambda hp, rb: (0, hp_n + hp)),
            pl.BlockSpec((n_ctx, LANES), lambda hp, rb: (0, 2 * hp_n + hp)),
            pl.BlockSpec((2, 3, tq, NA_WR * GRID_W), lambda hp, rb: (hp, 0, 0, 0)),
        ],
        out_specs=pl.BlockSpec((tq, LANES), lambda hp, rb: (rb, hp)),
        out_shape=jax.ShapeDtypeStruct((n_lat, D_MODEL), BF16),
        compiler_params=_cparams(("arbitrary", "arbitrary")),
        name="na_attention",
    )(qkv_lat, qkv_lat, qkv_lat, qkv_ctx, qkv_ctx, bias)


def _ctx_attn_kernel(q_ref, k_ref, v_ref, o_ref):
    qv = q_ref[...]
    lo = _lane_lt(qv.shape, NA_HEAD_DIM)
    zero = jnp.zeros_like(qv)
    outs = []
    for hh in range(2):
        qm = jnp.where(lo, qv, zero) if hh == 0 else jnp.where(lo, zero, qv)
        s = lax.dot_general(qm, k_ref[...], _NT, preferred_element_type=F32)
        outs.append(_softmax_pv([s], [v_ref[...]]))
    o_ref[...] = jnp.where(lo, outs[0], outs[1]).astype(o_ref.dtype)


def _ctx_attention(qkv_ctx):
    n_ctx = qkv_ctx.shape[0]
    hp_n = NA_HEADS // 2
    return pl.pallas_call(
        _ctx_attn_kernel,
        grid=(hp_n,),
        in_specs=[
            pl.BlockSpec((n_ctx, LANES), lambda hp: (0, hp)),
            pl.BlockSpec((n_ctx, LANES), lambda hp: (0, hp_n + hp)),
            pl.BlockSpec((n_ctx, LANES), lambda hp: (0, 2 * hp_n + hp)),
        ],
        out_specs=pl.BlockSpec((n_ctx, LANES), lambda hp: (0, hp)),
        out_shape=jax.ShapeDtypeStruct((n_ctx, D_MODEL), BF16),
        compiler_params=_cparams(("arbitrary",)),
        name="ctx_attention",
    )(qkv_ctx, qkv_ctx, qkv_ctx)


def _na_bias_table(rpb, rows):
    col = np.arange(GRID_W)
    col_start = np.clip(col - NA_COL_WIN // 2, 0, GRID_W - NA_COL_WIN)
    in_win = (col[None, :] >= col_start[:, None]) & (col[None, :] < col_start[:, None] + NA_COL_WIN)
    col_idx = np.clip(col[None, :] - col[:, None] + NA_COL_WIN - 1, 0, 2 * NA_COL_WIN - 2)
    col_bias = jnp.where(in_win, rpb.astype(F32)[:, :, col_idx], -jnp.inf)
    wr = min(NA_ROW_WIN, rows)
    idx = np.zeros((3, NA_RB, NA_WR), np.int32)
    valid = np.zeros((3, NA_RB, NA_WR), bool)
    for var, r0 in enumerate((0, NA_RB, rows - NA_RB)):
        wstart = int(np.clip(r0 - NA_ROW_WIN // 2, 0, rows - NA_WR))
        for qr in range(NA_RB):
            r = r0 + qr
            rs = int(np.clip(r - wr // 2, 0, rows - wr))
            for kj in range(NA_WR):
                kr = wstart + kj
                if rs <= kr < rs + wr:
                    valid[var, qr, kj] = True
                    idx[var, qr, kj] = kr - r + NA_ROW_WIN - 1
    t = col_bias[:, idx]
    t = jnp.where(valid[None, :, :, :, None, None], t, -jnp.inf)
    t = jnp.transpose(t, (0, 1, 2, 4, 3, 5))
    return t.reshape(NA_HEADS, 3, NA_RB * GRID_W, NA_WR * GRID_W)


def _ffn_kernel(xp_ref, x_ref, xn_ref, mul_ref, sh_ref, gate_ref, wup_ref, cw_ref, cb_ref, wdn_ref,
                o_ref, h_scr, acc_scr, act_scr, *, tm, nblk):
    _fill_h_with_halo(h_scr, xp_ref, x_ref, xn_ref, mul_ref[...], sh_ref[...], tm, nblk)
    hc = FFN_CHUNK
    nch = FFN_HIDDEN // hc
    def up(c):
        acc_scr[c] = jnp.dot(h_scr[...], wup_ref[c], preferred_element_type=F32)

    lead = 2
    for c in range(lead):
        up(c)
    for c in range(nch):
        if c + lead < nch:
            up(c + lead)
        a = _dwconv_from_scratch(acc_scr.at[c], cw_ref.at[c], cb_ref.at[c], FFN_CONV, tm)
        act_scr[c] = (_silu(a[:, :hc]) * a[:, hc:]).astype(BF16)
    y = jnp.dot(act_scr[0], wdn_ref[0], preferred_element_type=F32)
    for c in range(1, nch):
        y = y + jnp.dot(act_scr[c], wdn_ref[c], preferred_element_type=F32)
    o_ref[...] = x_ref[...] + gate_ref[...] * y


def _ffn(x, mul, shift, gate, wup_c, cw_c, cb_c, wdn_c):
    rows, d = x.shape
    nch, _, hc2 = wup_c.shape
    tm = _row_tile(rows, 512)
    nblk = rows // tm
    in_specs = _halo_specs(tm, d, rows) + [
        pl.BlockSpec((1, d), lambda i: (0, 0)),
        pl.BlockSpec((1, d), lambda i: (0, 0)),
        pl.BlockSpec((1, d), lambda i: (0, 0)),
        pl.BlockSpec((nch, d, hc2), lambda i: (0, 0, 0), pipeline_mode=pl.Buffered(1)),
        pl.BlockSpec((nch, FFN_CONV, hc2), lambda i: (0, 0, 0)),
        pl.BlockSpec((nch, 1, hc2), lambda i: (0, 0, 0)),
        pl.BlockSpec((nch, hc2 // 2, d), lambda i: (0, 0, 0), pipeline_mode=pl.Buffered(1)),
    ]
    return pl.pallas_call(
        functools.partial(_ffn_kernel, tm=tm, nblk=nblk),
        grid=(nblk,),
        in_specs=in_specs,
        out_specs=pl.BlockSpec((tm, d), lambda i: (i, 0)),
        out_shape=jax.ShapeDtypeStruct((rows, d), F32),
        scratch_shapes=[
            pltpu.VMEM((tm + 2 * HALO, d), BF16),
            pltpu.VMEM((nch, tm + 2 * HALO, hc2), F32),
            pltpu.VMEM((nch, tm, hc2 // 2), BF16),
        ],
        compiler_params=_cparams(("arbitrary",)),
        name="conv_ffn",
    )(x, x, x, mul, shift, gate, wup_c, cw_c, cb_c, wdn_c)


def _ffn_chunked(w):
    lead = w.shape[:-1]
    nch = FFN_HIDDEN // FFN_CHUNK
    w2 = w.reshape(lead + (2, nch, FFN_CHUNK))
    w2 = jnp.moveaxis(w2, -2, 0)
    return w2.reshape((nch,) + lead + (2 * FFN_CHUNK,))


def _pad_lanes(v, n):
    return jnp.pad(v, [(0, 0)] * (v.ndim - 1) + [(0, n - v.shape[-1])])


def kernel(x, c, ctx, c_ctx, norm_w, w_mod, b_mod, ssm_w_in, ssm_conv_w, ssm_conv_b, ssm_a_log, ssm_dt_bias, ssm_d_skip, ssm_norm_w, ssm_w_out, sgu_w_in, sgu_ln_w, sgu_ln_b, sgu_w_s, sgu_b_s, sgu_w_out, na_w_qkv, na_q_norm, na_k_norm, na_rpb, na_w_out, ffn_w_up, ffn_conv_w, ffn_conv_b, ffn_w_down):
    assert x.shape[0] == 1 and c.shape[0] == 1
    d = D_MODEL
    depth = w_mod.shape[0]
    x_lat = x[0]
    x_ctx = ctx[0]
    cond = jnp.zeros((8, d), F32).at[0].set(c[0]).at[1].set(c_ctx)
    mods = _mod_vectors(cond, w_mod, b_mod)

    def row(v):
        return v.reshape(1, -1)

    for i in range(depth):
        kind, j = i % N_MIXERS, i // N_MIXERS
        need_ctx = i < depth - 1
        ml = [row(mods[i, 0, k * d:(k + 1) * d]) for k in range(6)]
        mc = [row(mods[i, 1, k * d:(k + 1) * d]) for k in range(6)]
        nw0, nw1 = row(norm_w[i, 0]), row(norm_w[i, 1])
        mul_l, mul_c = nw0 * (1.0 + ml[1]), nw0 * (1.0 + mc[1])

        if kind == 0:
            w_in = ssm_w_in[j]
            w_zxbc = w_in[:, :SSM_ZXBC].astype(BF16)
            w_dt = w_in[:, SSM_ZXBC:]
            w_dt = jnp.concatenate([_pad_lanes(w_dt[:, :SSM_HEADS], LANES),
                                    _pad_lanes(w_dt[:, SSM_HEADS:], LANES)], axis=1).astype(BF16)
            cw = jnp.concatenate([jnp.zeros((SSM_CONV, SSM_INNER), F32), ssm_conv_w[j]], axis=1)
            cb = jnp.concatenate([jnp.zeros((SSM_INNER,), F32), ssm_conv_b[j]]).reshape(1, -1)
            a_neg = _pad_lanes(-jnp.exp(ssm_a_log[j]), LANES)
            dtb = _pad_lanes(ssm_dt_bias[j], LANES)
            dsk = row(jnp.repeat(ssm_d_skip[j, 0] + ssm_d_skip[j, 1], SSM_HEAD_DIM))
            gnw = row(ssm_norm_w[j])
            w_out = ssm_w_out[j].astype(BF16)
            h0 = jnp.zeros((SSM_GROUPS, SSM_STATE, SSM_HPG * SSM_HEAD_DIM), F32)

            def mixer(xs, mul, shift, hf0, hb0):
                zxbc, dt_raw = _ssm_in(xs, mul, shift, w_zxbc, w_dt, cw, cb)
                yf, hf = _ssd_scan(zxbc, dt_raw, dtb[0:1], a_neg[0:1], hf0, False)
                gn, hb = _ssd_scan(zxbc, dt_raw, dtb[1:2], a_neg[1:2], hb0, True,
                                   y_other=yf, d_skip=dsk, norm_w=gnw)
                return gn, hf, hb

            gn_c, hf, hb = mixer(x_ctx, mul_c, mc[0], h0, h0)
            gn_l, _, _ = mixer(x_lat, mul_l, ml[0], hf, hb)
            x_lat = _mm_res(gn_l, w_out, x_lat, ml[2])
            if need_ctx:
                x_ctx = _mm_res(gn_c, w_out, x_ctx, mc[2])
        elif kind == 1:
            w_in = sgu_w_in[j].astype(BF16)
            w_s = sgu_w_s[j].astype(BF16)
            gw = SGU_INNER // SGU_GROUPS
            bs_exp = jnp.repeat(sgu_b_s[j].T, gw, axis=1)
            w_out = sgu_w_out[j].astype(BF16)
            lnw, lnb = row(sgu_ln_w[j]), row(sgu_ln_b[j])
            z_l = _modmm(x_lat, mul_l, ml[0], w_in, "gelu")
            x_lat = _sgu_core(z_l, x_lat, lnw, lnb, w_s, bs_exp, w_out, ml[2])
            if need_ctx:
                z_c = _modmm(x_ctx, mul_c, mc[0], w_in, "gelu")
                x_ctx = _sgu_core(z_c, x_ctx, lnw, lnb, w_s, bs_exp, w_out, mc[2])
        else:
            w_qkv = na_w_qkv[j].astype(BF16)
            scale = NA_HEAD_DIM ** -0.5
            nw = jnp.concatenate([jnp.tile(na_q_norm[j] * scale, NA_HEADS),
                                  jnp.tile(na_k_norm[j], NA_HEADS),
                                  jnp.ones((d,), F32)]).reshape(1, -1)
            w_out = na_w_out[j].astype(BF16)
            bias = _na_bias_table(na_rpb[j], x_lat.shape[0] // GRID_W)
            qkv_c = _modmm(x_ctx, mul_c, mc[0], w_qkv, "qkv", nw)
            qkv_l = _modmm(x_lat, mul_l, ml[0], w_qkv, "qkv", nw)
            o_l = _na_attention(qkv_l, qkv_c, bias)
            x_lat = _mm_res(o_l, w_out, x_lat, ml[2])
            if need_ctx:
                o_c = _ctx_attention(qkv_c)
                x_ctx = _mm_res(o_c, w_out, x_ctx, mc[2])

        nch = FFN_HIDDEN // FFN_CHUNK
        wup_c = _ffn_chunked(ffn_w_up[i]).astype(BF16)
        cw_c = _ffn_chunked(ffn_conv_w[i])
        cb_c = _ffn_chunked(ffn_conv_b[i].reshape(1, -1))
        wdn_c = ffn_w_down[i].astype(BF16).reshape(nch, FFN_CHUNK, d)
        x_lat = _ffn(x_lat, nw1 * (1.0 + ml[4]), ml[3], ml[5], wup_c, cw_c, cb_c, wdn_c)
        if need_ctx:
            x_ctx = _ffn(x_ctx, nw1 * (1.0 + mc[4]), mc[3], mc[5], wup_c, cw_c, cb_c, wdn_c)
    return x_lat[None]
```

```python
import functools
import math

import numpy as np
import jax
import jax.numpy as jnp
from jax import lax
from jax.experimental import pallas as pl
from jax.experimental.pallas import tpu as pltpu

F32 = jnp.float32
BF16 = jnp.bfloat16
HIGHEST = lax.Precision.HIGHEST

D_MODEL = 1024
DEPTH = 4
N_MIXERS = 3
EPS = 1e-6
GRID_W = 64
SSM_INNER = 2 * D_MODEL
SSM_HEAD_DIM = 64
SSM_HEADS = SSM_INNER // SSM_HEAD_DIM
SSM_GROUPS = 8
SSM_HPG = SSM_HEADS // SSM_GROUPS
SSM_STATE = 128
SSM_CONV = 7
SSD_CHUNK = 128
SSM_GN = SSM_GROUPS * SSM_STATE
SSM_ZXBC = 2 * SSM_INNER + 2 * SSM_GN
SGU_INNER = 2 * D_MODEL
SGU_GROUPS = 8
TOKEN_CHUNK = 128
NA_HEAD_DIM = 64
NA_HEADS = D_MODEL // NA_HEAD_DIM
NA_ROW_WIN = 8
NA_COL_WIN = 16
FFN_HIDDEN = 2816
FFN_CONV = 3

LANES = 128
BF16_SUBLANES = 16
VMEM_LIMIT = 56 * 1024 * 1024

HALO = BF16_SUBLANES
FFN_CHUNK = 256
NA_RB = 4
NA_WR = NA_RB + NA_ROW_WIN


def _cparams(sem, flags=None):
    return pltpu.CompilerParams(dimension_semantics=sem, vmem_limit_bytes=VMEM_LIMIT, flags=flags)


def _row_tile(n, pref):
    t = min(n, pref)
    assert n % t == 0
    return t


def _sigmoid(v):
    return 1.0 / (1.0 + jnp.exp(-v))


def _silu(v):
    return v * _sigmoid(v)


def _modulate(x, mul, shift):
    ms = jnp.mean(x * x, axis=-1, keepdims=True)
    return x * lax.rsqrt(ms + EPS) * mul + shift


def _lane_lt(shape, n):
    return lax.broadcasted_iota(jnp.int32, shape, len(shape) - 1) < n


def _mod_kernel(c_ref, w_ref, b_ref, o_ref):
    s = _silu(c_ref[...])
    o_ref[0] = jnp.dot(s, w_ref[0], preferred_element_type=F32, precision=HIGHEST) + b_ref[0]


def _mod_vectors(cond, w_mod, b_mod):
    depth, d, n = w_mod.shape
    tn = 1536
    return pl.pallas_call(
        _mod_kernel,
        grid=(depth, n // tn),
        in_specs=[
            pl.BlockSpec((8, d), lambda i, j: (0, 0)),
            pl.BlockSpec((1, d, tn), lambda i, j: (i, 0, j)),
            pl.BlockSpec((1, 1, tn), lambda i, j: (i, 0, j)),
        ],
        out_specs=pl.BlockSpec((1, 8, tn), lambda i, j: (i, 0, j)),
        out_shape=jax.ShapeDtypeStruct((depth, 8, n), F32),
        compiler_params=_cparams(("arbitrary", "arbitrary")),
        name="mod_vectors",
    )(cond, w_mod, b_mod.reshape(depth, 1, n))


def _gelu_tanh(v):
    c = math.sqrt(2.0 / math.pi)
    return v * (0.5 * (1.0 + jnp.tanh(c * (v + 0.044715 * (v * v * v)))))


def _head_rmsnorm(blk, nw):
    lo = _lane_lt(blk.shape, NA_HEAD_DIM)
    sq = blk * blk
    s_lo = jnp.sum(jnp.where(lo, sq, 0.0), axis=-1, keepdims=True)
    s_hi = jnp.sum(jnp.where(lo, 0.0, sq), axis=-1, keepdims=True)
    r_lo = lax.rsqrt(s_lo * (1.0 / NA_HEAD_DIM) + EPS)
    r_hi = lax.rsqrt(s_hi * (1.0 / NA_HEAD_DIM) + EPS)
    return blk * jnp.where(lo, r_lo, r_hi) * nw


def _modmm_kernel(x_ref, mul_ref, sh_ref, w_ref, *rest, mode, tn):
    if mode == "qkv":
        nw_ref, o_ref, h_scr = rest
    else:
        o_ref, h_scr = rest
    j = pl.program_id(1)

    @pl.when(j == 0)
    def _():
        h_scr[...] = _modulate(x_ref[...], mul_ref[...], sh_ref[...]).astype(BF16)

    if mode == "gelu":
        acc = jnp.dot(h_scr[...], w_ref[...], preferred_element_type=F32)
        o_ref[...] = _gelu_tanh(acc).astype(o_ref.dtype)
    else:
        n_norm = 2 * D_MODEL // tn

        @pl.when(j < n_norm)
        def _():
            acc = jnp.dot(h_scr[...], w_ref[...], preferred_element_type=F32)
            for b in range(tn // LANES):
                sl = slice(b * LANES, (b + 1) * LANES)
                o_ref[:, sl] = _head_rmsnorm(acc[:, sl], nw_ref[:, sl]).astype(o_ref.dtype)

        @pl.when(j >= n_norm)
        def _():
            acc = jnp.dot(h_scr[...], w_ref[...], preferred_element_type=F32)
            o_ref[...] = acc.astype(o_ref.dtype)


def _modmm(x, mul, shift, w, mode, nw=None):
    rows, d = x.shape
    n = w.shape[1]
    tm = _row_tile(rows, 1024)
    tn = 512
    in_specs = [
        pl.BlockSpec((tm, d), lambda i, j: (i, 0)),
        pl.BlockSpec((1, d), lambda i, j: (0, 0)),
        pl.BlockSpec((1, d), lambda i, j: (0, 0)),
        pl.BlockSpec((d, tn), lambda i, j: (0, j)),
    ]
    args = [x, mul, shift, w]
    if mode == "qkv":
        in_specs.append(pl.BlockSpec((1, tn), lambda i, j: (0, j)))
        args.append(nw)
    return pl.pallas_call(
        functools.partial(_modmm_kernel, mode=mode, tn=tn),
        grid=(rows // tm, n // tn),
        in_specs=in_specs,
        out_specs=pl.BlockSpec((tm, tn), lambda i, j: (i, j)),
        out_shape=jax.ShapeDtypeStruct((rows, n), BF16),
        scratch_shapes=[pltpu.VMEM((tm, d), BF16)],
        compiler_params=_cparams(("arbitrary", "arbitrary")),
        name="modmm_" + mode,
    )(*args)


def _mmres_kernel(a_ref, w_ref, x_ref, g_ref, o_ref):
    y = jnp.dot(a_ref[...], w_ref[...], preferred_element_type=F32)
    o_ref[...] = x_ref[...] + g_ref[...] * y


def _mm_res(a, w, x, gate):
    rows, k = a.shape
    d = w.shape[1]
    tm = _row_tile(rows, 1024)
    return pl.pallas_call(
        _mmres_kernel,
        grid=(rows // tm,),
        in_specs=[
            pl.BlockSpec((tm, k), lambda i: (i, 0)),
            pl.BlockSpec((k, d), lambda i: (0, 0)),
            pl.BlockSpec((tm, d), lambda i: (i, 0)),
            pl.BlockSpec((1, d), lambda i: (0, 0)),
        ],
        out_specs=pl.BlockSpec((tm, d), lambda i: (i, 0)),
        out_shape=jax.ShapeDtypeStruct((rows, d), F32),
        compiler_params=_cparams(("arbitrary",)),
        name="mm_res",
    )(a, w, x, gate)


def _fill_h_with_halo(h_scr, xp_ref, x_ref, xn_ref, mul, sh, tm, nblk):
    i = pl.program_id(0)
    h_scr[HALO:HALO + tm, :] = _modulate(x_ref[...], mul, sh).astype(BF16)
    hp = jnp.where(i > 0, _modulate(xp_ref[...], mul, sh), 0.0)
    hn = jnp.where(i < nblk - 1, _modulate(xn_ref[...], mul, sh), 0.0)
    h_scr[0:HALO, :] = hp.astype(BF16)
    h_scr[HALO + tm:HALO + tm + HALO, :] = hn.astype(BF16)


def _halo_specs(tm, d, rows):
    per = tm // HALO
    last = rows // HALO - 1
    return [
        pl.BlockSpec((HALO, d), lambda i, *_: (jnp.maximum(i * per - 1, 0), 0)),
        pl.BlockSpec((tm, d), lambda i, *_: (i, 0)),
        pl.BlockSpec((HALO, d), lambda i, *_: (jnp.minimum((i + 1) * per, last), 0)),
    ]


def _dwconv_from_scratch(acc_scr, cw_ref, cb_ref, taps, tm):
    base = HALO - taps // 2
    y = cb_ref[...] + cw_ref[0:1, :] * acc_scr[pl.ds(base, tm), :]
    for k in range(1, taps):
        y = y + cw_ref[k:k + 1, :] * acc_scr[pl.ds(base + k, tm), :]
    return y


def _dwconv_rolled(a, cw_ref, cb_ref, col, taps, tm):
    n, c = a.shape
    sub = 8
    a3 = a.reshape(n // sub, sub, c)
    sidx = lax.broadcasted_iota(jnp.int32, (1, sub, c), 1)
    rolled = {}
    y = cb_ref[:, col]
    for k in range(taps):
        off = HALO - taps // 2 + k
        q = off % sub
        g0 = (off - q) // sub
        if q == 0:
            tap = a3[g0:g0 + tm // sub]
        else:
            if q not in rolled:
                rolled[q] = pltpu.roll(a3, sub - q, axis=1)
            r = rolled[q]
            tap = jnp.where(sidx < sub - q, r[g0:g0 + tm // sub], r[g0 + 1:g0 + 1 + tm // sub])
        y = y + cw_ref[k:k + 1, col] * tap.reshape(tm, c)
    return y


def _ssm_in_kernel(xp_ref, x_ref, xn_ref, mul_ref, sh_ref, w_ref, wdt_ref, cw_ref, cb_ref,
                   o_ref, dt_ref, h_scr, *, tm, nblk, tn):
    _fill_h_with_halo(h_scr, xp_ref, x_ref, xn_ref, mul_ref[...], sh_ref[...], tm, nblk)
    h_mid = h_scr[HALO:HALO + tm, :]
    dt_ref[...] = jnp.dot(h_mid, wdt_ref[...], preferred_element_type=F32)
    for j in range(SSM_INNER // tn):
        col = slice(j * tn, (j + 1) * tn)
        o_ref[:, col] = jnp.dot(h_mid, w_ref[:, col], preferred_element_type=F32).astype(o_ref.dtype)
    for j in range(SSM_INNER // tn, SSM_ZXBC // tn):
        col = slice(j * tn, (j + 1) * tn)
        a = jnp.dot(h_scr[...], w_ref[:, col], preferred_element_type=F32)
        y = _dwconv_rolled(a, cw_ref, cb_ref, col, SSM_CONV, tm)
        o_ref[:, col] = _silu(y).astype(o_ref.dtype)


def _ssm_in(x, mul, shift, w_zxbc, w_dt, conv_w, conv_b):
    rows, d = x.shape
    n = w_zxbc.shape[1]
    tm = _row_tile(rows, 512)
    tn = 256
    nblk = rows // tm
    ndt = w_dt.shape[1]
    in_specs = _halo_specs(tm, d, rows) + [
        pl.BlockSpec((1, d), lambda i: (0, 0)),
        pl.BlockSpec((1, d), lambda i: (0, 0)),
        pl.BlockSpec((d, n), lambda i: (0, 0), pipeline_mode=pl.Buffered(1)),
        pl.BlockSpec((d, ndt), lambda i: (0, 0)),
        pl.BlockSpec((SSM_CONV, n), lambda i: (0, 0)),
        pl.BlockSpec((1, n), lambda i: (0, 0)),
    ]
    return pl.pallas_call(
        functools.partial(_ssm_in_kernel, tm=tm, nblk=nblk, tn=tn),
        grid=(nblk,),
        in_specs=in_specs,
        out_specs=[
            pl.BlockSpec((tm, n), lambda i: (i, 0)),
            pl.BlockSpec((tm, ndt), lambda i: (i, 0)),
        ],
        out_shape=[
            jax.ShapeDtypeStruct((rows, n), BF16),
            jax.ShapeDtypeStruct((rows, ndt), F32),
        ],
        scratch_shapes=[pltpu.VMEM((tm + 2 * HALO, d), BF16)],
        compiler_params=_cparams(("arbitrary",)),
        name="ssm_in",
    )(x, x, x, mul, shift, w_zxbc, w_dt, conv_w, conv_b)


def _softplus(v):
    return jnp.maximum(v, 0.0) + jnp.log1p(jnp.exp(-jnp.abs(v)))


LOG2E = 1.4426950408889634


def _scan_mask(q, rev):
    row = lax.broadcasted_iota(jnp.int32, (q, q), 0)
    col = lax.broadcasted_iota(jnp.int32, (q, q), 1)
    return (col >= row) if rev else (row >= col)


def _hi_lo(v):
    hi = v.astype(BF16)
    return hi, (v - hi.astype(F32)).astype(BF16)


def _ssd_prep_kernel(dt_ref, dtb_ref, an_ref, cum2_ref, rowt_ref, ef_ref, etot_ref, *, nck):
    q = SSD_CHUNK
    for d in range(2):
        rev = d == 1
        tri = _scan_mask(q, rev).astype(F32)
        for ck in range(nck):
            rs = slice(ck * q, (ck + 1) * q)
            dt = _softplus(dt_ref[rs, d * LANES:(d + 1) * LANES] + dtb_ref[d:d + 1, :])
            d_a = dt * an_ref[d:d + 1, :]
            cum = jnp.dot(tri, d_a, preferred_element_type=F32, precision=HIGHEST)
            tot = cum[0:1, :] if rev else cum[q - 1:q, :]
            cum2_ref[d, rs, :] = cum * LOG2E
            rowt_ref[d, ck] = ((cum - jnp.log(dt)) * LOG2E).T
            parts = _hi_lo(jnp.exp(cum)) + _hi_lo(dt * jnp.exp(tot - cum))
            for k, part in enumerate(parts):
                ef_ref[d, rs, k * LANES:(k + 1) * LANES] = part
            etot_ref[d, ck] = jnp.broadcast_to(jnp.exp(tot), (8, LANES))


def _ssd_prep(dt_raw, dt_bias, a_neg):
    rows = dt_raw.shape[0]
    q = SSD_CHUNK
    tm = _row_tile(rows, 1024)
    nck = tm // q
    nc = rows // q
    col_spec = pl.BlockSpec((2, tm, LANES), lambda i: (0, i, 0))
    return pl.pallas_call(
        functools.partial(_ssd_prep_kernel, nck=nck),
        grid=(rows // tm,),
        in_specs=[
            pl.BlockSpec((tm, 2 * LANES), lambda i: (i, 0)),
            pl.BlockSpec((2, LANES), lambda i: (0, 0)),
            pl.BlockSpec((2, LANES), lambda i: (0, 0)),
        ],
        out_specs=[
            col_spec,
            pl.BlockSpec((2, nck, LANES, q), lambda i: (0, i, 0, 0)),
            pl.BlockSpec((2, tm, 4 * LANES), lambda i: (0, i, 0)),
            pl.BlockSpec((2, nck, 8, LANES), lambda i: (0, i, 0, 0)),
        ],
        out_shape=[
            jax.ShapeDtypeStruct((2, rows, LANES), F32),
            jax.ShapeDtypeStruct((2, nc, LANES, q), F32),
            jax.ShapeDtypeStruct((2, rows, 4 * LANES), BF16),
            jax.ShapeDtypeStruct((2, nc, 8, LANES), F32),
        ],
        compiler_params=_cparams(("arbitrary",)),
        name="ssd_prep",
    )(dt_raw, dt_bias, a_neg)


def _ssd_kernel(*refs, rev, epilogue):
    if epilogue:
        (x_ref, b_ref, c_ref, cum2_ref, rowt_ref, ef_ref, etot_ref, sel_ref, h0_ref,
         z_ref, yf_ref, dsk_ref, nw_ref, y_ref, hf_ref, h_scr) = refs
    else:
        (x_ref, b_ref, c_ref, cum2_ref, rowt_ref, ef_ref, etot_ref, sel_ref, h0_ref,
         y_ref, hf_ref, h_scr) = refs
    q = SSD_CHUNK
    hd = SSM_HEAD_DIM
    step = pl.program_id(0)

    @pl.when(step == 0)
    def _():
        h_scr[...] = h0_ref[...]

    tmask = _scan_mask(q, rev)
    cum2 = cum2_ref[0]
    rowt = rowt_ref[0, 0]
    etot = etot_ref[0, 0][0:1, :]
    lo = _lane_lt((q, LANES), hd)
    lo1 = _lane_lt((1, LANES), hd)
    gw = SSM_HPG * hd

    for g in range(SSM_GROUPS):
        bg = b_ref[:, g * SSM_STATE:(g + 1) * SSM_STATE]
        cg = c_ref[:, g * SSM_STATE:(g + 1) * SSM_STATE]
        cb = lax.dot_general(cg, bg, (((1,), (1,)), ((), ())), preferred_element_type=F32)
        h_t = h_scr[g]
        esc_g = jnp.dot(ef_ref[0, :, 0:2 * LANES], sel_ref[g], preferred_element_type=F32)
        fsc_g = jnp.dot(ef_ref[0, :, 2 * LANES:4 * LANES], sel_ref[g], preferred_element_type=F32)
        yoff = jnp.dot(cg, h_t.astype(BF16), preferred_element_type=F32) * esc_g
        wg = (x_ref[:, g * gw:(g + 1) * gw].astype(F32) * fsc_g).astype(BF16)
        dec_rows = []
        gated = []
        for pair in range(SSM_HPG // 2):
            ms = []
            for rr in range(2):
                h = g * SSM_HPG + 2 * pair + rr
                seg = cum2[:, h:h + 1] - rowt[h:h + 1, :]
                ms.append((cb * jnp.exp2(jnp.where(tmask, seg, -jnp.inf))).astype(BF16))
            h0i = g * SSM_HPG + 2 * pair
            c0 = g * SSM_HPG * hd + pair * LANES
            xp = x_ref[:, c0:c0 + LANES]
            zero = jnp.zeros_like(xp)
            xcat = jnp.concatenate([jnp.where(lo, xp, zero), jnp.where(lo, zero, xp)], axis=0)
            ydiag = jnp.dot(jnp.concatenate(ms, axis=1), xcat, preferred_element_type=F32)
            y_pair = ydiag + yoff[:, pair * LANES:(pair + 1) * LANES]
            dec_rows.append(jnp.where(lo1, etot[:, h0i:h0i + 1], etot[:, h0i + 1:h0i + 2]))
            if epilogue:
                yt = y_pair + yf_ref[:, c0:c0 + LANES].astype(F32) + dsk_ref[:, c0:c0 + LANES] * xp.astype(F32)
                gated.append(yt * _silu(z_ref[:, c0:c0 + LANES].astype(F32)))
            else:
                y_ref[:, c0:c0 + LANES] = y_pair.astype(y_ref.dtype)
        if epilogue:
            ssq = sum(jnp.sum(v * v, axis=-1, keepdims=True) for v in gated)
            rinv = lax.rsqrt(ssq * (1.0 / (SSM_HPG * hd)) + EPS)
            for pair, v in enumerate(gated):
                c0 = g * SSM_HPG * hd + pair * LANES
                y_ref[:, c0:c0 + LANES] = (v * rinv * nw_ref[:, c0:c0 + LANES]).astype(y_ref.dtype)
        st = lax.dot_general(bg, wg, (((0,), (0,)), ((), ())), preferred_element_type=F32)
        h_scr[g] = h_t * jnp.concatenate(dec_rows, axis=1) + st

    @pl.when(step == pl.num_programs(0) - 1)
    def _():
        hf_ref[...] = h_scr[...]


def _head_select():
    head = np.arange(LANES)[None, :, None]
    col = np.arange(SSM_HPG * SSM_HEAD_DIM)[None, None, :]
    g = np.arange(SSM_GROUPS)[:, None, None]
    one = (head == g * SSM_HPG + col // SSM_HEAD_DIM).astype(np.float32)
    return jnp.asarray(np.concatenate([one, one], axis=1), dtype=BF16)


def _ssd_scan(zxbc, prep, h0, rev, y_other=None, d_skip=None, norm_w=None):
    rows = zxbc.shape[0]
    q = SSD_CHUNK
    nc = rows // q
    epilogue = y_other is not None
    d = 1 if rev else 0

    def cidx(s):
        return (nc - 1 - s) if rev else s

    col_spec = pl.BlockSpec((1, q, LANES), lambda s: (d, cidx(s), 0))
    in_specs = [
        pl.BlockSpec((q, SSM_INNER), lambda s: (cidx(s), 1)),
        pl.BlockSpec((q, SSM_GN), lambda s: (cidx(s), 2 * SSM_INNER // SSM_GN)),
        pl.BlockSpec((q, SSM_GN), lambda s: (cidx(s), 2 * SSM_INNER // SSM_GN + 1)),
        col_spec,
        pl.BlockSpec((1, 1, LANES, q), lambda s: (d, cidx(s), 0, 0)),
        pl.BlockSpec((1, q, 4 * LANES), lambda s: (d, cidx(s), 0)),
        pl.BlockSpec((1, 1, 8, LANES), lambda s: (d, cidx(s), 0, 0)),
        pl.BlockSpec((SSM_GROUPS, 2 * LANES, SSM_HPG * SSM_HEAD_DIM), lambda s: (0, 0, 0)),
        pl.BlockSpec((SSM_GROUPS, SSM_STATE, SSM_HPG * SSM_HEAD_DIM), lambda s: (0, 0, 0)),
    ]
    cum2, rowt, ef, etot = prep
    args = [zxbc, zxbc, zxbc, cum2, rowt, ef, etot, _head_select(), h0]
    if epilogue:
        in_specs += [
            pl.BlockSpec((q, SSM_INNER), lambda s: (cidx(s), 0)),
            pl.BlockSpec((q, SSM_INNER), lambda s: (cidx(s), 0)),
            pl.BlockSpec((1, SSM_INNER), lambda s: (0, 0)),
            pl.BlockSpec((1, SSM_INNER), lambda s: (0, 0)),
        ]
        args += [zxbc, y_other, d_skip, norm_w]
    state_shape = (SSM_GROUPS, SSM_STATE, SSM_HPG * SSM_HEAD_DIM)
    return pl.pallas_call(
        functools.partial(_ssd_kernel, rev=rev, epilogue=epilogue),
        grid=(nc,),
        in_specs=in_specs,
        out_specs=[
            pl.BlockSpec((q, SSM_INNER), lambda s: (cidx(s), 0)),
            pl.BlockSpec(state_shape, lambda s: (0, 0, 0)),
        ],
        out_shape=[
            jax.ShapeDtypeStruct((rows, SSM_INNER), BF16),
            jax.ShapeDtypeStruct(state_shape, F32),
        ],
        scratch_shapes=[pltpu.VMEM(state_shape, F32)],
        compiler_params=_cparams(("arbitrary",)),
        name="ssd_bwd" if rev else "ssd_fwd",
    )(*args)


def _sgu_kernel(u_ref, v_ref, x_ref, lnw_ref, lnb_ref, ws_ref, bs_ref, wo_ref, g_ref, o_ref,
                vn_scr, uv_scr, *, tm):
    v = v_ref[...].astype(F32)
    mu = jnp.mean(v, axis=-1, keepdims=True)
    vc = v - mu
    var = jnp.mean(vc * vc, axis=-1, keepdims=True)
    vn_scr[...] = (vc * lax.rsqrt(var + EPS) * lnw_ref[...] + lnb_ref[...]).astype(BF16)
    gw = SGU_INNER // SGU_GROUPS
    for ch in range(tm // TOKEN_CHUNK):
        rs = slice(ch * TOKEN_CHUNK, (ch + 1) * TOKEN_CHUNK)
        for g in range(SGU_GROUPS):
            cs = slice(g * gw, (g + 1) * gw)
            sv = jnp.dot(ws_ref[g], vn_scr[rs, cs], preferred_element_type=F32) + bs_ref[:, cs]
            uv_scr[rs, cs] = (u_ref[rs, cs].astype(F32) * sv).astype(BF16)
    y = jnp.dot(uv_scr[...], wo_ref[...], preferred_element_type=F32)
    o_ref[...] = x_ref[...] + g_ref[...] * y


def _sgu_core(z, x, ln_w, ln_b, w_s, bs_exp, w_out, gate):
    rows, d = x.shape
    e = SGU_INNER
    tm = _row_tile(rows, 512)
    return pl.pallas_call(
        functools.partial(_sgu_kernel, tm=tm),
        grid=(rows // tm,),
        in_specs=[
            pl.BlockSpec((tm, e), lambda i: (i, 0)),
            pl.BlockSpec((tm, e), lambda i: (i, 1)),
            pl.BlockSpec((tm, d), lambda i: (i, 0)),
            pl.BlockSpec((1, e), lambda i: (0, 0)),
            pl.BlockSpec((1, e), lambda i: (0, 0)),
            pl.BlockSpec((SGU_GROUPS, TOKEN_CHUNK, TOKEN_CHUNK), lambda i: (0, 0, 0)),
            pl.BlockSpec((TOKEN_CHUNK, e), lambda i: (0, 0)),
            pl.BlockSpec((e, d), lambda i: (0, 0)),
            pl.BlockSpec((1, d), lambda i: (0, 0)),
        ],
        out_specs=pl.BlockSpec((tm, d), lambda i: (i, 0)),
        out_shape=jax.ShapeDtypeStruct((rows, d), F32),
        scratch_shapes=[pltpu.VMEM((tm, e), BF16), pltpu.VMEM((tm, e), BF16)],
        compiler_params=_cparams(("arbitrary",)),
        name="sgu_core",
    )(z, z, x, ln_w, ln_b, w_s, bs_exp, w_out, gate)


def _softmax_pv(s_list, v_list):
    m = s_list[0].max(axis=-1, keepdims=True)
    for s in s_list[1:]:
        m = jnp.maximum(m, s.max(axis=-1, keepdims=True))
    ps = [jnp.exp(s - m) for s in s_list]
    den = sum(p.sum(axis=-1, keepdims=True) for p in ps)
    o = sum(jnp.dot(p.astype(BF16), v, preferred_element_type=F32) for p, v in zip(ps, v_list))
    return o * (1.0 / den)


_NT = (((1,), (1,)), ((), ()))


def _na_kernel(q_ref, k_ref, v_ref, kc_ref, vc_ref, bias_ref, o_ref, *, rows, nblk):
    rb = pl.program_id(1)
    wstart = jnp.clip(rb * NA_RB - NA_ROW_WIN // 2, 0, rows - NA_WR)
    variant = jnp.where(rb == 0, 0, jnp.where(rb == nblk - 1, 2, 1))
    koff = pl.multiple_of(wstart * GRID_W, GRID_W)
    kw = k_ref[pl.ds(koff, NA_WR * GRID_W), :]
    vw = v_ref[pl.ds(koff, NA_WR * GRID_W), :]
    kc = kc_ref[...]
    vc = vc_ref[...]
    qv = q_ref[...]
    lo = _lane_lt(qv.shape, NA_HEAD_DIM)
    zero = jnp.zeros_like(qv)
    outs = []
    for hh in range(2):
        qm = jnp.where(lo, qv, zero) if hh == 0 else jnp.where(lo, zero, qv)
        s_win = lax.dot_general(qm, kw, _NT, preferred_element_type=F32) + bias_ref[hh, variant]
        s_ctx = lax.dot_general(qm, kc, _NT, preferred_element_type=F32)
        outs.append(_softmax_pv([s_win, s_ctx], [vw, vc]))
    o_ref[...] = jnp.where(lo, outs[0], outs[1]).astype(o_ref.dtype)


def _na_attention(qkv_lat, qkv_ctx, bias):
    n_lat = qkv_lat.shape[0]
    n_ctx = qkv_ctx.shape[0]
    rows = n_lat // GRID_W
    nblk = rows // NA_RB
    assert rows >= NA_WR and rows % NA_RB == 0
    hp_n = NA_HEADS // 2
    tq = NA_RB * GRID_W
    return pl.pallas_call(
        functools.partial(_na_kernel, rows=rows, nblk=nblk),
        grid=(hp_n, nblk),
        in_specs=[
            pl.BlockSpec((tq, LANES), lambda hp, rb: (rb, hp)),
            pl.BlockSpec((n_lat, LANES), lambda hp, rb: (0, hp_n + hp)),
            pl.BlockSpec((n_lat, LANES), lambda hp, rb: (0, 2 * hp_n + hp)),
            pl.BlockSpec((n_ctx, LANES), lambda hp, rb: (0, hp_n + hp)),
            pl.BlockSpec((n_ctx, LANES), lambda hp, rb: (0, 2 * hp_n + hp)),
            pl.BlockSpec((2, 3, tq, NA_WR * GRID_W), lambda hp, rb: (hp, 0, 0, 0)),
        ],
        out_specs=pl.BlockSpec((tq, LANES), lambda hp, rb: (rb, hp)),
        out_shape=jax.ShapeDtypeStruct((n_lat, D_MODEL), BF16),
        compiler_params=_cparams(("arbitrary", "arbitrary")),
        name="na_attention",
    )(qkv_lat, qkv_lat, qkv_lat, qkv_ctx, qkv_ctx, bias)


def _ctx_attn_kernel(q_ref, k_ref, v_ref, o_ref):
    qv = q_ref[...]
    lo = _lane_lt(qv.shape, NA_HEAD_DIM)
    zero = jnp.zeros_like(qv)
    outs = []
    for hh in range(2):
        qm = jnp.where(lo, qv, zero) if hh == 0 else jnp.where(lo, zero, qv)
        s = lax.dot_general(qm, k_ref[...], _NT, preferred_element_type=F32)
        outs.append(_softmax_pv([s], [v_ref[...]]))
    o_ref[...] = jnp.where(lo, outs[0], outs[1]).astype(o_ref.dtype)


def _ctx_attention(qkv_ctx):
    n_ctx = qkv_ctx.shape[0]
    hp_n = NA_HEADS // 2
    return pl.pallas_call(
        _ctx_attn_kernel,
        grid=(hp_n,),
        in_specs=[
            pl.BlockSpec((n_ctx, LANES), lambda hp: (0, hp)),
            pl.BlockSpec((n_ctx, LANES), lambda hp: (0, hp_n + hp)),
            pl.BlockSpec((n_ctx, LANES), lambda hp: (0, 2 * hp_n + hp)),
        ],
        out_specs=pl.BlockSpec((n_ctx, LANES), lambda hp: (0, hp)),
        out_shape=jax.ShapeDtypeStruct((n_ctx, D_MODEL), BF16),
        compiler_params=_cparams(("arbitrary",)),
        name="ctx_attention",
    )(qkv_ctx, qkv_ctx, qkv_ctx)


def _na_bias_table(rpb, rows):
    col = np.arange(GRID_W)
    col_start = np.clip(col - NA_COL_WIN // 2, 0, GRID_W - NA_COL_WIN)
    in_win = (col[None, :] >= col_start[:, None]) & (col[None, :] < col_start[:, None] + NA_COL_WIN)
    col_idx = np.clip(col[None, :] - col[:, None] + NA_COL_WIN - 1, 0, 2 * NA_COL_WIN - 2)
    col_bias = jnp.where(in_win, rpb.astype(F32)[:, :, col_idx], -jnp.inf)
    wr = min(NA_ROW_WIN, rows)
    idx = np.zeros((3, NA_RB, NA_WR), np.int32)
    valid = np.zeros((3, NA_RB, NA_WR), bool)
    for var, r0 in enumerate((0, NA_RB, rows - NA_RB)):
        wstart = int(np.clip(r0 - NA_ROW_WIN // 2, 0, rows - NA_WR))
        for qr in range(NA_RB):
            r = r0 + qr
            rs = int(np.clip(r - wr // 2, 0, rows - wr))
            for kj in range(NA_WR):
                kr = wstart + kj
                if rs <= kr < rs + wr:
                    valid[var, qr, kj] = True
                    idx[var, qr, kj] = kr - r + NA_ROW_WIN - 1
    t = col_bias[:, idx]
    t = jnp.where(valid[None, :, :, :, None, None], t, -jnp.inf)
    t = jnp.transpose(t, (0, 1, 2, 4, 3, 5))
    return t.reshape(NA_HEADS, 3, NA_RB * GRID_W, NA_WR * GRID_W)


def _ffn_kernel(xp_ref, x_ref, xn_ref, mul_ref, sh_ref, gate_ref, wup_ref, cw_ref, cb_ref, wdn_ref,
                o_ref, h_scr, acc_scr, act_scr, *, tm, nblk):
    _fill_h_with_halo(h_scr, xp_ref, x_ref, xn_ref, mul_ref[...], sh_ref[...], tm, nblk)
    hc = FFN_CHUNK
    nch = FFN_HIDDEN // hc
    def up(c):
        acc_scr[c] = jnp.dot(h_scr[...], wup_ref[c], preferred_element_type=F32)

    lead = 2
    for c in range(lead):
        up(c)
    for c in range(nch):
        if c + lead < nch:
            up(c + lead)
        a = _dwconv_from_scratch(acc_scr.at[c], cw_ref.at[c], cb_ref.at[c], FFN_CONV, tm)
        act_scr[c] = (_silu(a[:, :hc]) * a[:, hc:]).astype(BF16)
    y = jnp.dot(act_scr[0], wdn_ref[0], preferred_element_type=F32)
    for c in range(1, nch):
        y = y + jnp.dot(act_scr[c], wdn_ref[c], preferred_element_type=F32)
    o_ref[...] = x_ref[...] + gate_ref[...] * y


def _ffn(x, mul, shift, gate, wup_c, cw_c, cb_c, wdn_c):
    rows, d = x.shape
    nch, _, hc2 = wup_c.shape
    tm = _row_tile(rows, 512)
    nblk = rows // tm
    in_specs = _halo_specs(tm, d, rows) + [
        pl.BlockSpec((1, d), lambda i: (0, 0)),
        pl.BlockSpec((1, d), lambda i: (0, 0)),
        pl.BlockSpec((1, d), lambda i: (0, 0)),
        pl.BlockSpec((nch, d, hc2), lambda i: (0, 0, 0), pipeline_mode=pl.Buffered(1)),
        pl.BlockSpec((nch, FFN_CONV, hc2), lambda i: (0, 0, 0)),
        pl.BlockSpec((nch, 1, hc2), lambda i: (0, 0, 0)),
        pl.BlockSpec((nch, hc2 // 2, d), lambda i: (0, 0, 0), pipeline_mode=pl.Buffered(1)),
    ]
    return pl.pallas_call(
        functools.partial(_ffn_kernel, tm=tm, nblk=nblk),
        grid=(nblk,),
        in_specs=in_specs,
        out_specs=pl.BlockSpec((tm, d), lambda i: (i, 0)),
        out_shape=jax.ShapeDtypeStruct((rows, d), F32),
        scratch_shapes=[
            pltpu.VMEM((tm + 2 * HALO, d), BF16),
            pltpu.VMEM((nch, tm + 2 * HALO, hc2), F32),
            pltpu.VMEM((nch, tm, hc2 // 2), BF16),
        ],
        compiler_params=_cparams(("arbitrary",)),
        name="conv_ffn",
    )(x, x, x, mul, shift, gate, wup_c, cw_c, cb_c, wdn_c)


def _ffn_chunked(w):
    lead = w.shape[:-1]
    nch = FFN_HIDDEN // FFN_CHUNK
    w2 = w.reshape(lead + (2, nch, FFN_CHUNK))
    w2 = jnp.moveaxis(w2, -2, 0)
    return w2.reshape((nch,) + lead + (2 * FFN_CHUNK,))


def _pad_lanes(v, n):
    return jnp.pad(v, [(0, 0)] * (v.ndim - 1) + [(0, n - v.shape[-1])])


def kernel(x, c, ctx, c_ctx, norm_w, w_mod, b_mod, ssm_w_in, ssm_conv_w, ssm_conv_b, ssm_a_log, ssm_dt_bias, ssm_d_skip, ssm_norm_w, ssm_w_out, sgu_w_in, sgu_ln_w, sgu_ln_b, sgu_w_s, sgu_b_s, sgu_w_out, na_w_qkv, na_q_norm, na_k_norm, na_rpb, na_w_out, ffn_w_up, ffn_conv_w, ffn_conv_b, ffn_w_down):
    assert x.shape[0] == 1 and c.shape[0] == 1
    d = D_MODEL
    depth = w_mod.shape[0]
    x_lat = x[0]
    x_ctx = ctx[0]
    cond = jnp.zeros((8, d), F32).at[0].set(c[0]).at[1].set(c_ctx)
    mods = _mod_vectors(cond, w_mod, b_mod)

    def row(v):
        return v.reshape(1, -1)

    for i in range(depth):
        kind, j = i % N_MIXERS, i // N_MIXERS
        need_ctx = i < depth - 1
        ml = [row(mods[i, 0, k * d:(k + 1) * d]) for k in range(6)]
        mc = [row(mods[i, 1, k * d:(k + 1) * d]) for k in range(6)]
        nw0, nw1 = row(norm_w[i, 0]), row(norm_w[i, 1])
        mul_l, mul_c = nw0 * (1.0 + ml[1]), nw0 * (1.0 + mc[1])

        if kind == 0:
            w_in = ssm_w_in[j]
            w_zxbc = w_in[:, :SSM_ZXBC].astype(BF16)
            w_dt = w_in[:, SSM_ZXBC:]
            w_dt = jnp.concatenate([_pad_lanes(w_dt[:, :SSM_HEADS], LANES),
                                    _pad_lanes(w_dt[:, SSM_HEADS:], LANES)], axis=1).astype(BF16)
            cw = jnp.concatenate([jnp.zeros((SSM_CONV, SSM_INNER), F32), ssm_conv_w[j]], axis=1)
            cb = jnp.concatenate([jnp.zeros((SSM_INNER,), F32), ssm_conv_b[j]]).reshape(1, -1)
            a_neg = _pad_lanes(-jnp.exp(ssm_a_log[j]), LANES)
            dtb = _pad_lanes(ssm_dt_bias[j], LANES)
            dsk = row(jnp.repeat(ssm_d_skip[j, 0] + ssm_d_skip[j, 1], SSM_HEAD_DIM))
            gnw = row(ssm_norm_w[j])
            w_out = ssm_w_out[j].astype(BF16)
            h0 = jnp.zeros((SSM_GROUPS, SSM_STATE, SSM_HPG * SSM_HEAD_DIM), F32)

            def mixer(xs, mul, shift, hf0, hb0):
                zxbc, dt_raw = _ssm_in(xs, mul, shift, w_zxbc, w_dt, cw, cb)
                prep = _ssd_prep(dt_raw, dtb, a_neg)
                yf, hf = _ssd_scan(zxbc, prep, hf0, False)
                gn, hb = _ssd_scan(zxbc, prep, hb0, True, y_other=yf, d_skip=dsk, norm_w=gnw)
                return gn, hf, hb

            gn_c, hf, hb = mixer(x_ctx, mul_c, mc[0], h0, h0)
            gn_l, _, _ = mixer(x_lat, mul_l, ml[0], hf, hb)
            x_lat = _mm_res(gn_l, w_out, x_lat, ml[2])
            if need_ctx:
                x_ctx = _mm_res(gn_c, w_out, x_ctx, mc[2])
        elif kind == 1:
            w_in = sgu_w_in[j].astype(BF16)
            w_s = sgu_w_s[j].astype(BF16)
            gw = SGU_INNER // SGU_GROUPS
            bs_exp = jnp.repeat(sgu_b_s[j].T, gw, axis=1)
            w_out = sgu_w_out[j].astype(BF16)
            lnw, lnb = row(sgu_ln_w[j]), row(sgu_ln_b[j])
            z_l = _modmm(x_lat, mul_l, ml[0], w_in, "gelu")
            x_lat = _sgu_core(z_l, x_lat, lnw, lnb, w_s, bs_exp, w_out, ml[2])
            if need_ctx:
                z_c = _modmm(x_ctx, mul_c, mc[0], w_in, "gelu")
                x_ctx = _sgu_core(z_c, x_ctx, lnw, lnb, w_s, bs_exp, w_out, mc[2])
        else:
            w_qkv = na_w_qkv[j].astype(BF16)
            scale = NA_HEAD_DIM ** -0.5
            nw = jnp.concatenate([jnp.tile(na_q_norm[j] * scale, NA_HEADS),
                                  jnp.tile(na_k_norm[j], NA_HEADS),
                                  jnp.ones((d,), F32)]).reshape(1, -1)
            w_out = na_w_out[j].astype(BF16)
            bias = _na_bias_table(na_rpb[j], x_lat.shape[0] // GRID_W)
            qkv_c = _modmm(x_ctx, mul_c, mc[0], w_qkv, "qkv", nw)
            qkv_l = _modmm(x_lat, mul_l, ml[0], w_qkv, "qkv", nw)
            o_l = _na_attention(qkv_l, qkv_c, bias)
            x_lat = _mm_res(o_l, w_out, x_lat, ml[2])
            if need_ctx:
                o_c = _ctx_attention(qkv_c)
                x_ctx = _mm_res(o_c, w_out, x_ctx, mc[2])

        nch = FFN_HIDDEN // FFN_CHUNK
        wup_c = _ffn_chunked(ffn_w_up[i]).astype(BF16)
        cw_c = _ffn_chunked(ffn_conv_w[i])
        cb_c = _ffn_chunked(ffn_conv_b[i].reshape(1, -1))
        wdn_c = ffn_w_down[i].astype(BF16).reshape(nch, FFN_CHUNK, d)
        x_lat = _ffn(x_lat, nw1 * (1.0 + ml[4]), ml[3], ml[5], wup_c, cw_c, cb_c, wdn_c)
        if need_ctx:
            x_ctx = _ffn(x_ctx, nw1 * (1.0 + mc[4]), mc[3], mc[5], wup_c, cw_c, cb_c, wdn_c)
    return x_lat[None]
```

```python
import functools
import math

import numpy as np
import jax
import jax.numpy as jnp
from jax import lax
from jax.experimental import pallas as pl
from jax.experimental.pallas import tpu as pltpu

F32 = jnp.float32
BF16 = jnp.bfloat16
HIGHEST = lax.Precision.HIGHEST

D_MODEL = 1024
DEPTH = 4
N_MIXERS = 3
EPS = 1e-6
GRID_W = 64
SSM_INNER = 2 * D_MODEL
SSM_HEAD_DIM = 64
SSM_HEADS = SSM_INNER // SSM_HEAD_DIM
SSM_GROUPS = 8
SSM_HPG = SSM_HEADS // SSM_GROUPS
SSM_STATE = 128
SSM_CONV = 7
SSD_CHUNK = 128
SSM_GN = SSM_GROUPS * SSM_STATE
SSM_ZXBC = 2 * SSM_INNER + 2 * SSM_GN
SGU_INNER = 2 * D_MODEL
SGU_GROUPS = 8
TOKEN_CHUNK = 128
NA_HEAD_DIM = 64
NA_HEADS = D_MODEL // NA_HEAD_DIM
NA_ROW_WIN = 8
NA_COL_WIN = 16
FFN_HIDDEN = 2816
FFN_CONV = 3

LANES = 128
BF16_SUBLANES = 16
VMEM_LIMIT = 56 * 1024 * 1024

HALO = BF16_SUBLANES
FFN_CHUNK = 256
NA_RB = 4
NA_WR = NA_RB + NA_ROW_WIN


def _cparams(sem, flags=None):
    return pltpu.CompilerParams(dimension_semantics=sem, vmem_limit_bytes=VMEM_LIMIT, flags=flags)


def _row_tile(n, pref):
    t = min(n, pref)
    assert n % t == 0
    return t


def _sigmoid(v):
    return 1.0 / (1.0 + jnp.exp(-v))


def _silu(v):
    return v * _sigmoid(v)


def _modulate(x, mul, shift):
    ms = jnp.mean(x * x, axis=-1, keepdims=True)
    return x * lax.rsqrt(ms + EPS) * mul + shift


def _lane_lt(shape, n):
    return lax.broadcasted_iota(jnp.int32, shape, len(shape) - 1) < n


def _mod_kernel(c_ref, w_ref, b_ref, o_ref):
    s = _silu(c_ref[...])
    o_ref[0] = jnp.dot(s, w_ref[0], preferred_element_type=F32, precision=HIGHEST) + b_ref[0]


def _mod_vectors(cond, w_mod, b_mod):
    depth, d, n = w_mod.shape
    tn = 1536
    return pl.pallas_call(
        _mod_kernel,
        grid=(depth, n // tn),
        in_specs=[
            pl.BlockSpec((8, d), lambda i, j: (0, 0)),
            pl.BlockSpec((1, d, tn), lambda i, j: (i, 0, j)),
            pl.BlockSpec((1, 1, tn), lambda i, j: (i, 0, j)),
        ],
        out_specs=pl.BlockSpec((1, 8, tn), lambda i, j: (i, 0, j)),
        out_shape=jax.ShapeDtypeStruct((depth, 8, n), F32),
        compiler_params=_cparams(("arbitrary", "arbitrary")),
        name="mod_vectors",
    )(cond, w_mod, b_mod.reshape(depth, 1, n))


def _gelu_tanh(v):
    c = math.sqrt(2.0 / math.pi)
    return v * (0.5 * (1.0 + jnp.tanh(c * (v + 0.044715 * (v * v * v)))))


def _head_rmsnorm(blk, nw):
    lo = _lane_lt(blk.shape, NA_HEAD_DIM)
    sq = blk * blk
    s_lo = jnp.sum(jnp.where(lo, sq, 0.0), axis=-1, keepdims=True)
    s_hi = jnp.sum(jnp.where(lo, 0.0, sq), axis=-1, keepdims=True)
    r_lo = lax.rsqrt(s_lo * (1.0 / NA_HEAD_DIM) + EPS)
    r_hi = lax.rsqrt(s_hi * (1.0 / NA_HEAD_DIM) + EPS)
    return blk * jnp.where(lo, r_lo, r_hi) * nw


def _modmm_kernel(x_ref, mul_ref, sh_ref, w_ref, *rest, mode, tn):
    if mode == "qkv":
        nw_ref, o_ref, h_scr = rest
    else:
        o_ref, h_scr = rest
    h_scr[...] = _modulate(x_ref[...], mul_ref[...], sh_ref[...]).astype(BF16)
    n = w_ref.shape[1]
    for j in range(n // tn):
        col = slice(j * tn, (j + 1) * tn)
        acc = jnp.dot(h_scr[...], w_ref[:, col], preferred_element_type=F32)
        if mode == "gelu":
            o_ref[:, col] = _gelu_tanh(acc).astype(o_ref.dtype)
        elif j < 2 * D_MODEL // tn:
            for b in range(tn // LANES):
                sl = slice(j * tn + b * LANES, j * tn + (b + 1) * LANES)
                o_ref[:, sl] = _head_rmsnorm(acc[:, b * LANES:(b + 1) * LANES], nw_ref[:, sl]).astype(o_ref.dtype)
        else:
            o_ref[:, col] = acc.astype(o_ref.dtype)


def _modmm(x, mul, shift, w, mode, nw=None):
    rows, d = x.shape
    n = w.shape[1]
    tm = _row_tile(rows, 512)
    tn = 256
    in_specs = [
        pl.BlockSpec((tm, d), lambda i: (i, 0)),
        pl.BlockSpec((1, d), lambda i: (0, 0)),
        pl.BlockSpec((1, d), lambda i: (0, 0)),
        pl.BlockSpec((d, n), lambda i: (0, 0), pipeline_mode=pl.Buffered(1)),
    ]
    args = [x, mul, shift, w]
    if mode == "qkv":
        in_specs.append(pl.BlockSpec((1, n), lambda i: (0, 0)))
        args.append(nw)
    return pl.pallas_call(
        functools.partial(_modmm_kernel, mode=mode, tn=tn),
        grid=(rows // tm,),
        in_specs=in_specs,
        out_specs=pl.BlockSpec((tm, n), lambda i: (i, 0)),
        out_shape=jax.ShapeDtypeStruct((rows, n), BF16),
        scratch_shapes=[pltpu.VMEM((tm, d), BF16)],
        compiler_params=_cparams(("arbitrary",)),
        name="modmm_" + mode,
    )(*args)


def _mmres_kernel(a_ref, w_ref, x_ref, g_ref, o_ref):
    y = jnp.dot(a_ref[...], w_ref[...], preferred_element_type=F32)
    o_ref[...] = x_ref[...] + g_ref[...] * y


def _mm_res(a, w, x, gate):
    rows, k = a.shape
    d = w.shape[1]
    tm = _row_tile(rows, 1024)
    return pl.pallas_call(
        _mmres_kernel,
        grid=(rows // tm,),
        in_specs=[
            pl.BlockSpec((tm, k), lambda i: (i, 0)),
            pl.BlockSpec((k, d), lambda i: (0, 0)),
            pl.BlockSpec((tm, d), lambda i: (i, 0)),
            pl.BlockSpec((1, d), lambda i: (0, 0)),
        ],
        out_specs=pl.BlockSpec((tm, d), lambda i: (i, 0)),
        out_shape=jax.ShapeDtypeStruct((rows, d), F32),
        compiler_params=_cparams(("arbitrary",)),
        name="mm_res",
    )(a, w, x, gate)


def _fill_h_with_halo(h_scr, xp_ref, x_ref, xn_ref, mul, sh, tm, nblk):
    i = pl.program_id(0)
    h_scr[HALO:HALO + tm, :] = _modulate(x_ref[...], mul, sh).astype(BF16)
    hp = jnp.where(i > 0, _modulate(xp_ref[...], mul, sh), 0.0)
    hn = jnp.where(i < nblk - 1, _modulate(xn_ref[...], mul, sh), 0.0)
    h_scr[0:HALO, :] = hp.astype(BF16)
    h_scr[HALO + tm:HALO + tm + HALO, :] = hn.astype(BF16)


def _halo_specs(tm, d, rows):
    per = tm // HALO
    last = rows // HALO - 1
    return [
        pl.BlockSpec((HALO, d), lambda i, *_: (jnp.maximum(i * per - 1, 0), 0)),
        pl.BlockSpec((tm, d), lambda i, *_: (i, 0)),
        pl.BlockSpec((HALO, d), lambda i, *_: (jnp.minimum((i + 1) * per, last), 0)),
    ]


def _dwconv_from_scratch(acc_scr, cw_ref, cb_ref, taps, tm):
    base = HALO - taps // 2
    y = cb_ref[...] + cw_ref[0:1, :] * acc_scr[pl.ds(base, tm), :]
    for k in range(1, taps):
        y = y + cw_ref[k:k + 1, :] * acc_scr[pl.ds(base + k, tm), :]
    return y


def _dwconv_rolled(a, cw_ref, cb_ref, col, taps, tm):
    n, c = a.shape
    sub = 8
    a3 = a.reshape(n // sub, sub, c)
    sidx = lax.broadcasted_iota(jnp.int32, (1, sub, c), 1)
    rolled = {}
    y = cb_ref[:, col]
    for k in range(taps):
        off = HALO - taps // 2 + k
        q = off % sub
        g0 = (off - q) // sub
        if q == 0:
            tap = a3[g0:g0 + tm // sub]
        else:
            if q not in rolled:
                rolled[q] = pltpu.roll(a3, sub - q, axis=1)
            r = rolled[q]
            tap = jnp.where(sidx < sub - q, r[g0:g0 + tm // sub], r[g0 + 1:g0 + 1 + tm // sub])
        y = y + cw_ref[k:k + 1, col] * tap.reshape(tm, c)
    return y


def _ssm_in_kernel(xp_ref, x_ref, xn_ref, mul_ref, sh_ref, w_ref, wdt_ref, cw_ref, cb_ref,
                   o_ref, dt_ref, h_scr, *, tm, nblk, tn):
    _fill_h_with_halo(h_scr, xp_ref, x_ref, xn_ref, mul_ref[...], sh_ref[...], tm, nblk)
    h_mid = h_scr[HALO:HALO + tm, :]
    dt_ref[...] = jnp.dot(h_mid, wdt_ref[...], preferred_element_type=F32)
    for j in range(SSM_INNER // tn):
        col = slice(j * tn, (j + 1) * tn)
        o_ref[:, col] = jnp.dot(h_mid, w_ref[:, col], preferred_element_type=F32).astype(o_ref.dtype)
    for j in range(SSM_INNER // tn, SSM_ZXBC // tn):
        col = slice(j * tn, (j + 1) * tn)
        a = jnp.dot(h_scr[...], w_ref[:, col], preferred_element_type=F32)
        y = _dwconv_rolled(a, cw_ref, cb_ref, col, SSM_CONV, tm)
        o_ref[:, col] = _silu(y).astype(o_ref.dtype)


def _ssm_in(x, mul, shift, w_zxbc, w_dt, conv_w, conv_b):
    rows, d = x.shape
    n = w_zxbc.shape[1]
    tm = _row_tile(rows, 512)
    tn = 256
    nblk = rows // tm
    ndt = w_dt.shape[1]
    in_specs = _halo_specs(tm, d, rows) + [
        pl.BlockSpec((1, d), lambda i: (0, 0)),
        pl.BlockSpec((1, d), lambda i: (0, 0)),
        pl.BlockSpec((d, n), lambda i: (0, 0), pipeline_mode=pl.Buffered(1)),
        pl.BlockSpec((d, ndt), lambda i: (0, 0)),
        pl.BlockSpec((SSM_CONV, n), lambda i: (0, 0)),
        pl.BlockSpec((1, n), lambda i: (0, 0)),
    ]
    return pl.pallas_call(
        functools.partial(_ssm_in_kernel, tm=tm, nblk=nblk, tn=tn),
        grid=(nblk,),
        in_specs=in_specs,
        out_specs=[
            pl.BlockSpec((tm, n), lambda i: (i, 0)),
            pl.BlockSpec((tm, ndt), lambda i: (i, 0)),
        ],
        out_shape=[
            jax.ShapeDtypeStruct((rows, n), BF16),
            jax.ShapeDtypeStruct((rows, ndt), F32),
        ],
        scratch_shapes=[pltpu.VMEM((tm + 2 * HALO, d), BF16)],
        compiler_params=_cparams(("arbitrary",)),
        name="ssm_in",
    )(x, x, x, mul, shift, w_zxbc, w_dt, conv_w, conv_b)


def _softplus(v):
    return jnp.maximum(v, 0.0) + jnp.log1p(jnp.exp(-jnp.abs(v)))


LOG2E = 1.4426950408889634


def _scan_mask(q, rev):
    row = lax.broadcasted_iota(jnp.int32, (q, q), 0)
    col = lax.broadcasted_iota(jnp.int32, (q, q), 1)
    return (col >= row) if rev else (row >= col)


def _hi_lo(v):
    hi = v.astype(BF16)
    return hi, (v - hi.astype(F32)).astype(BF16)


def _ssd_prep_kernel(dt_ref, dtb_ref, an_ref, cum2_ref, rowt_ref, ef_ref, etot_ref, *, nck):
    q = SSD_CHUNK
    for d in range(2):
        rev = d == 1
        tri = _scan_mask(q, rev).astype(F32)
        for ck in range(nck):
            rs = slice(ck * q, (ck + 1) * q)
            dt = _softplus(dt_ref[rs, d * LANES:(d + 1) * LANES] + dtb_ref[d:d + 1, :])
            d_a = dt * an_ref[d:d + 1, :]
            cum = jnp.dot(tri, d_a, preferred_element_type=F32, precision=HIGHEST)
            tot = cum[0:1, :] if rev else cum[q - 1:q, :]
            cum2_ref[d, rs, :] = cum * LOG2E
            rowt_ref[d, ck] = ((cum - jnp.log(dt)) * LOG2E).T
            parts = _hi_lo(jnp.exp(cum)) + _hi_lo(dt * jnp.exp(tot - cum))
            for k, part in enumerate(parts):
                ef_ref[d, rs, k * LANES:(k + 1) * LANES] = part
            etot_ref[d, ck] = jnp.broadcast_to(jnp.exp(tot), (8, LANES))


def _ssd_prep(dt_raw, dt_bias, a_neg):
    rows = dt_raw.shape[0]
    q = SSD_CHUNK
    tm = _row_tile(rows, 1024)
    nck = tm // q
    nc = rows // q
    col_spec = pl.BlockSpec((2, tm, LANES), lambda i: (0, i, 0))
    return pl.pallas_call(
        functools.partial(_ssd_prep_kernel, nck=nck),
        grid=(rows // tm,),
        in_specs=[
            pl.BlockSpec((tm, 2 * LANES), lambda i: (i, 0)),
            pl.BlockSpec((2, LANES), lambda i: (0, 0)),
            pl.BlockSpec((2, LANES), lambda i: (0, 0)),
        ],
        out_specs=[
            col_spec,
            pl.BlockSpec((2, nck, LANES, q), lambda i: (0, i, 0, 0)),
            pl.BlockSpec((2, tm, 4 * LANES), lambda i: (0, i, 0)),
            pl.BlockSpec((2, nck, 8, LANES), lambda i: (0, i, 0, 0)),
        ],
        out_shape=[
            jax.ShapeDtypeStruct((2, rows, LANES), F32),
            jax.ShapeDtypeStruct((2, nc, LANES, q), F32),
            jax.ShapeDtypeStruct((2, rows, 4 * LANES), BF16),
            jax.ShapeDtypeStruct((2, nc, 8, LANES), F32),
        ],
        compiler_params=_cparams(("arbitrary",)),
        name="ssd_prep",
    )(dt_raw, dt_bias, a_neg)


def _ssd_kernel(*refs, rev, epilogue):
    if epilogue:
        (x_ref, b_ref, c_ref, cum2_ref, rowt_ref, ef_ref, etot_ref, sel_ref, h0_ref,
         z_ref, yf_ref, dsk_ref, nw_ref, y_ref, hf_ref, h_scr) = refs
    else:
        (x_ref, b_ref, c_ref, cum2_ref, rowt_ref, ef_ref, etot_ref, sel_ref, h0_ref,
         y_ref, hf_ref, h_scr) = refs
    q = SSD_CHUNK
    hd = SSM_HEAD_DIM
    step = pl.program_id(0)

    @pl.when(step == 0)
    def _():
        h_scr[...] = h0_ref[...]

    tmask = _scan_mask(q, rev)
    cum2 = cum2_ref[0]
    rowt = rowt_ref[0, 0]
    etot = etot_ref[0, 0][0:1, :]
    lo = _lane_lt((q, LANES), hd)
    lo1 = _lane_lt((1, LANES), hd)
    gw = SSM_HPG * hd

    for g in range(SSM_GROUPS):
        bg = b_ref[:, g * SSM_STATE:(g + 1) * SSM_STATE]
        cg = c_ref[:, g * SSM_STATE:(g + 1) * SSM_STATE]
        cb = lax.dot_general(cg, bg, (((1,), (1,)), ((), ())), preferred_element_type=F32)
        h_t = h_scr[g]
        esc_g = jnp.dot(ef_ref[0, :, 0:2 * LANES], sel_ref[g], preferred_element_type=F32)
        fsc_g = jnp.dot(ef_ref[0, :, 2 * LANES:4 * LANES], sel_ref[g], preferred_element_type=F32)
        yoff = jnp.dot(cg, h_t.astype(BF16), preferred_element_type=F32) * esc_g
        wg = (x_ref[:, g * gw:(g + 1) * gw].astype(F32) * fsc_g).astype(BF16)
        dec_rows = []
        gated = []
        for pair in range(SSM_HPG // 2):
            ms = []
            for rr in range(2):
                h = g * SSM_HPG + 2 * pair + rr
                seg = cum2[:, h:h + 1] - rowt[h:h + 1, :]
                ms.append((cb * jnp.exp2(jnp.where(tmask, seg, -jnp.inf))).astype(BF16))
            h0i = g * SSM_HPG + 2 * pair
            c0 = g * SSM_HPG * hd + pair * LANES
            xp = x_ref[:, c0:c0 + LANES]
            zero = jnp.zeros_like(xp)
            xcat = jnp.concatenate([jnp.where(lo, xp, zero), jnp.where(lo, zero, xp)], axis=0)
            ydiag = jnp.dot(jnp.concatenate(ms, axis=1), xcat, preferred_element_type=F32)
            y_pair = ydiag + yoff[:, pair * LANES:(pair + 1) * LANES]
            dec_rows.append(jnp.where(lo1, etot[:, h0i:h0i + 1], etot[:, h0i + 1:h0i + 2]))
            if epilogue:
                yt = y_pair + yf_ref[:, c0:c0 + LANES].astype(F32) + dsk_ref[:, c0:c0 + LANES] * xp.astype(F32)
                gated.append(yt * _silu(z_ref[:, c0:c0 + LANES].astype(F32)))
            else:
                y_ref[:, c0:c0 + LANES] = y_pair.astype(y_ref.dtype)
        if epilogue:
            ssq = sum(jnp.sum(v * v, axis=-1, keepdims=True) for v in gated)
            rinv = lax.rsqrt(ssq * (1.0 / (SSM_HPG * hd)) + EPS)
            for pair, v in enumerate(gated):
                c0 = g * SSM_HPG * hd + pair * LANES
                y_ref[:, c0:c0 + LANES] = (v * rinv * nw_ref[:, c0:c0 + LANES]).astype(y_ref.dtype)
        st = lax.dot_general(bg, wg, (((0,), (0,)), ((), ())), preferred_element_type=F32)
        h_scr[g] = h_t * jnp.concatenate(dec_rows, axis=1) + st

    @pl.when(step == pl.num_programs(0) - 1)
    def _():
        hf_ref[...] = h_scr[...]


def _head_select():
    head = np.arange(LANES)[None, :, None]
    col = np.arange(SSM_HPG * SSM_HEAD_DIM)[None, None, :]
    g = np.arange(SSM_GROUPS)[:, None, None]
    one = (head == g * SSM_HPG + col // SSM_HEAD_DIM).astype(np.float32)
    return jnp.asarray(np.concatenate([one, one], axis=1), dtype=BF16)


def _ssd_scan(zxbc, prep, h0, rev, y_other=None, d_skip=None, norm_w=None):
    rows = zxbc.shape[0]
    q = SSD_CHUNK
    nc = rows // q
    epilogue = y_other is not None
    d = 1 if rev else 0

    def cidx(s):
        return (nc - 1 - s) if rev else s

    col_spec = pl.BlockSpec((1, q, LANES), lambda s: (d, cidx(s), 0))
    in_specs = [
        pl.BlockSpec((q, SSM_INNER), lambda s: (cidx(s), 1)),
        pl.BlockSpec((q, SSM_GN), lambda s: (cidx(s), 2 * SSM_INNER // SSM_GN)),
        pl.BlockSpec((q, SSM_GN), lambda s: (cidx(s), 2 * SSM_INNER // SSM_GN + 1)),
        col_spec,
        pl.BlockSpec((1, 1, LANES, q), lambda s: (d, cidx(s), 0, 0)),
        pl.BlockSpec((1, q, 4 * LANES), lambda s: (d, cidx(s), 0)),
        pl.BlockSpec((1, 1, 8, LANES), lambda s: (d, cidx(s), 0, 0)),
        pl.BlockSpec((SSM_GROUPS, 2 * LANES, SSM_HPG * SSM_HEAD_DIM), lambda s: (0, 0, 0)),
        pl.BlockSpec((SSM_GROUPS, SSM_STATE, SSM_HPG * SSM_HEAD_DIM), lambda s: (0, 0, 0)),
    ]
    cum2, rowt, ef, etot = prep
    args = [zxbc, zxbc, zxbc, cum2, rowt, ef, etot, _head_select(), h0]
    if epilogue:
        in_specs += [
            pl.BlockSpec((q, SSM_INNER), lambda s: (cidx(s), 0)),
            pl.BlockSpec((q, SSM_INNER), lambda s: (cidx(s), 0)),
            pl.BlockSpec((1, SSM_INNER), lambda s: (0, 0)),
            pl.BlockSpec((1, SSM_INNER), lambda s: (0, 0)),
        ]
        args += [zxbc, y_other, d_skip, norm_w]
    state_shape = (SSM_GROUPS, SSM_STATE, SSM_HPG * SSM_HEAD_DIM)
    return pl.pallas_call(
        functools.partial(_ssd_kernel, rev=rev, epilogue=epilogue),
        grid=(nc,),
        in_specs=in_specs,
        out_specs=[
            pl.BlockSpec((q, SSM_INNER), lambda s: (cidx(s), 0)),
            pl.BlockSpec(state_shape, lambda s: (0, 0, 0)),
        ],
        out_shape=[
            jax.ShapeDtypeStruct((rows, SSM_INNER), BF16),
            jax.ShapeDtypeStruct(state_shape, F32),
        ],
        scratch_shapes=[pltpu.VMEM(state_shape, F32)],
        compiler_params=_cparams(("arbitrary",)),
        name="ssd_bwd" if rev else "ssd_fwd",
    )(*args)


def _sgu_kernel(u_ref, v_ref, x_ref, lnw_ref, lnb_ref, ws_ref, bs_ref, wo_ref, g_ref, o_ref,
                vn_scr, uv_scr, *, tm):
    v = v_ref[...].astype(F32)
    mu = jnp.mean(v, axis=-1, keepdims=True)
    vc = v - mu
    var = jnp.mean(vc * vc, axis=-1, keepdims=True)
    vn_scr[...] = (vc * lax.rsqrt(var + EPS) * lnw_ref[...] + lnb_ref[...]).astype(BF16)
    gw = SGU_INNER // SGU_GROUPS
    for ch in range(tm // TOKEN_CHUNK):
        rs = slice(ch * TOKEN_CHUNK, (ch + 1) * TOKEN_CHUNK)
        for g in range(SGU_GROUPS):
            cs = slice(g * gw, (g + 1) * gw)
            sv = jnp.dot(ws_ref[g], vn_scr[rs, cs], preferred_element_type=F32) + bs_ref[:, cs]
            uv_scr[rs, cs] = (u_ref[rs, cs].astype(F32) * sv).astype(BF16)
    y = jnp.dot(uv_scr[...], wo_ref[...], preferred_element_type=F32)
    o_ref[...] = x_ref[...] + g_ref[...] * y


def _sgu_core(z, x, ln_w, ln_b, w_s, bs_exp, w_out, gate):
    rows, d = x.shape
    e = SGU_INNER
    tm = _row_tile(rows, 512)
    return pl.pallas_call(
        functools.partial(_sgu_kernel, tm=tm),
        grid=(rows // tm,),
        in_specs=[
            pl.BlockSpec((tm, e), lambda i: (i, 0)),
            pl.BlockSpec((tm, e), lambda i: (i, 1)),
            pl.BlockSpec((tm, d), lambda i: (i, 0)),
            pl.BlockSpec((1, e), lambda i: (0, 0)),
            pl.BlockSpec((1, e), lambda i: (0, 0)),
            pl.BlockSpec((SGU_GROUPS, TOKEN_CHUNK, TOKEN_CHUNK), lambda i: (0, 0, 0)),
            pl.BlockSpec((TOKEN_CHUNK, e), lambda i: (0, 0)),
            pl.BlockSpec((e, d), lambda i: (0, 0)),
            pl.BlockSpec((1, d), lambda i: (0, 0)),
        ],
        out_specs=pl.BlockSpec((tm, d), lambda i: (i, 0)),
        out_shape=jax.ShapeDtypeStruct((rows, d), F32),
        scratch_shapes=[pltpu.VMEM((tm, e), BF16), pltpu.VMEM((tm, e), BF16)],
        compiler_params=_cparams(("arbitrary",)),
        name="sgu_core",
    )(z, z, x, ln_w, ln_b, w_s, bs_exp, w_out, gate)


def _softmax_pv(s_list, v_list):
    m = s_list[0].max(axis=-1, keepdims=True)
    for s in s_list[1:]:
        m = jnp.maximum(m, s.max(axis=-1, keepdims=True))
    ps = [jnp.exp(s - m) for s in s_list]
    den = sum(p.sum(axis=-1, keepdims=True) for p in ps)
    o = sum(jnp.dot(p.astype(BF16), v, preferred_element_type=F32) for p, v in zip(ps, v_list))
    return o * (1.0 / den)


_NT = (((1,), (1,)), ((), ()))


def _na_kernel(q_ref, k_ref, v_ref, kc_ref, vc_ref, bias_ref, o_ref, *, rows, nblk, nsub):
    kc = kc_ref[...]
    vc = vc_ref[...]
    tq = NA_RB * GRID_W
    lo = _lane_lt((tq, LANES), NA_HEAD_DIM)
    for sb in range(nsub):
        rb = pl.program_id(1) * nsub + sb
        wstart = jnp.clip(rb * NA_RB - NA_ROW_WIN // 2, 0, rows - NA_WR)
        variant = jnp.where(rb == 0, 0, jnp.where(rb == nblk - 1, 2, 1))
        koff = pl.multiple_of(wstart * GRID_W, GRID_W)
        kw = k_ref[pl.ds(koff, NA_WR * GRID_W), :]
        vw = v_ref[pl.ds(koff, NA_WR * GRID_W), :]
        qv = q_ref[sb * tq:(sb + 1) * tq, :]
        zero = jnp.zeros_like(qv)
        outs = []
        for hh in range(2):
            qm = jnp.where(lo, qv, zero) if hh == 0 else jnp.where(lo, zero, qv)
            s_win = lax.dot_general(qm, kw, _NT, preferred_element_type=F32) + bias_ref[hh, variant]
            s_ctx = lax.dot_general(qm, kc, _NT, preferred_element_type=F32)
            outs.append(_softmax_pv([s_win, s_ctx], [vw, vc]))
        o_ref[sb * tq:(sb + 1) * tq, :] = jnp.where(lo, outs[0], outs[1]).astype(o_ref.dtype)


def _na_attention(qkv_lat, qkv_ctx, bias):
    n_lat = qkv_lat.shape[0]
    n_ctx = qkv_ctx.shape[0]
    rows = n_lat // GRID_W
    nblk = rows // NA_RB
    assert rows >= NA_WR and rows % NA_RB == 0
    hp_n = NA_HEADS // 2
    nsub = 2 if nblk % 2 == 0 else 1
    tq = nsub * NA_RB * GRID_W
    return pl.pallas_call(
        functools.partial(_na_kernel, rows=rows, nblk=nblk, nsub=nsub),
        grid=(hp_n, nblk // nsub),
        in_specs=[
            pl.BlockSpec((tq, LANES), lambda hp, rb: (rb, hp)),
            pl.BlockSpec((n_lat, LANES), lambda hp, rb: (0, hp_n + hp)),
            pl.BlockSpec((n_lat, LANES), lambda hp, rb: (0, 2 * hp_n + hp)),
            pl.BlockSpec((n_ctx, LANES), lambda hp, rb: (0, hp_n + hp)),
            pl.BlockSpec((n_ctx, LANES), lambda hp, rb: (0, 2 * hp_n + hp)),
            pl.BlockSpec((2, 3, NA_RB * GRID_W, NA_WR * GRID_W), lambda hp, rb: (hp, 0, 0, 0)),
        ],
        out_specs=pl.BlockSpec((tq, LANES), lambda hp, rb: (rb, hp)),
        out_shape=jax.ShapeDtypeStruct((n_lat, D_MODEL), BF16),
        compiler_params=_cparams(("arbitrary", "arbitrary")),
        name="na_attention",
    )(qkv_lat, qkv_lat, qkv_lat, qkv_ctx, qkv_ctx, bias)


def _ctx_attn_kernel(q_ref, k_ref, v_ref, o_ref):
    qv = q_ref[...]
    lo = _lane_lt(qv.shape, NA_HEAD_DIM)
    zero = jnp.zeros_like(qv)
    outs = []
    for hh in range(2):
        qm = jnp.where(lo, qv, zero) if hh == 0 else jnp.where(lo, zero, qv)
        s = lax.dot_general(qm, k_ref[...], _NT, preferred_element_type=F32)
        outs.append(_softmax_pv([s], [v_ref[...]]))
    o_ref[...] = jnp.where(lo, outs[0], outs[1]).astype(o_ref.dtype)


def _ctx_attention(qkv_ctx):
    n_ctx = qkv_ctx.shape[0]
    hp_n = NA_HEADS // 2
    return pl.pallas_call(
        _ctx_attn_kernel,
        grid=(hp_n,),
        in_specs=[
            pl.BlockSpec((n_ctx, LANES), lambda hp: (0, hp)),
            pl.BlockSpec((n_ctx, LANES), lambda hp: (0, hp_n + hp)),
            pl.BlockSpec((n_ctx, LANES), lambda hp: (0, 2 * hp_n + hp)),
        ],
        out_specs=pl.BlockSpec((n_ctx, LANES), lambda hp: (0, hp)),
        out_shape=jax.ShapeDtypeStruct((n_ctx, D_MODEL), BF16),
        compiler_params=_cparams(("arbitrary",)),
        name="ctx_attention",
    )(qkv_ctx, qkv_ctx, qkv_ctx)


def _na_bias_table(rpb, rows):
    col = np.arange(GRID_W)
    col_start = np.clip(col - NA_COL_WIN // 2, 0, GRID_W - NA_COL_WIN)
    in_win = (col[None, :] >= col_start[:, None]) & (col[None, :] < col_start[:, None] + NA_COL_WIN)
    w = GRID_W
    edge = w - NA_COL_WIN
    rp = jnp.pad(rpb.astype(F32), ((0, 0), (0, 0), (edge, edge)))
    col_bias = jnp.stack([rp[:, :, w - 1 - qc:2 * w - 1 - qc] for qc in range(w)], axis=2)
    col_bias = jnp.where(in_win, col_bias, -jnp.inf)
    cbt = jnp.transpose(col_bias, (0, 2, 1, 3)).reshape(NA_HEADS, w, (2 * NA_ROW_WIN - 1) * w)
    wr = NA_ROW_WIN
    assert rows >= NA_WR
    variants = []
    for r0 in (0, NA_RB, rows - NA_RB):
        wstart = int(np.clip(r0 - NA_ROW_WIN // 2, 0, rows - NA_WR))
        blocks = []
        for qr in range(NA_RB):
            r = r0 + qr
            rs = int(np.clip(r - wr // 2, 0, rows - wr))
            first = rs - wstart
            a0 = rs - r + NA_ROW_WIN - 1
            blk = cbt[:, :, a0 * w:(a0 + wr) * w]
            blocks.append(jnp.pad(blk, ((0, 0), (0, 0), (first * w, (NA_WR - first - wr) * w)),
                                  constant_values=-jnp.inf))
        variants.append(jnp.concatenate(blocks, axis=1))
    return jnp.stack(variants, axis=1)


def _ffn_kernel(xp_ref, x_ref, xn_ref, mul_ref, sh_ref, gate_ref, wup_ref, cw_ref, cb_ref, wdn_ref,
                o_ref, h_scr, acc_scr, act_scr, *, tm, nblk):
    _fill_h_with_halo(h_scr, xp_ref, x_ref, xn_ref, mul_ref[...], sh_ref[...], tm, nblk)
    hc = FFN_CHUNK
    nch = FFN_HIDDEN // hc
    def up(c):
        for half in range(2):
            col = slice(half * FFN_HIDDEN + c * hc, half * FFN_HIDDEN + (c + 1) * hc)
            acc_scr[c, :, half * hc:(half + 1) * hc] = jnp.dot(
                h_scr[...], wup_ref[:, col], preferred_element_type=F32)

    lead = 2
    for c in range(lead):
        up(c)
    for c in range(nch):
        if c + lead < nch:
            up(c + lead)
        a = _dwconv_from_scratch(acc_scr.at[c], cw_ref.at[c], cb_ref.at[c], FFN_CONV, tm)
        act_scr[c] = (_silu(a[:, :hc]) * a[:, hc:]).astype(BF16)
    y = jnp.dot(act_scr[0], wdn_ref[0], preferred_element_type=F32)
    for c in range(1, nch):
        y = y + jnp.dot(act_scr[c], wdn_ref[c], preferred_element_type=F32)
    o_ref[...] = x_ref[...] + gate_ref[...] * y


def _ffn(x, mul, shift, gate, wup, cw_c, cb_c, wdn_c):
    rows, d = x.shape
    nch, _, hc2 = cw_c.shape
    tm = _row_tile(rows, 512)
    nblk = rows // tm
    in_specs = _halo_specs(tm, d, rows) + [
        pl.BlockSpec((1, d), lambda i: (0, 0)),
        pl.BlockSpec((1, d), lambda i: (0, 0)),
        pl.BlockSpec((1, d), lambda i: (0, 0)),
        pl.BlockSpec((d, 2 * FFN_HIDDEN), lambda i: (0, 0), pipeline_mode=pl.Buffered(1)),
        pl.BlockSpec((nch, FFN_CONV, hc2), lambda i: (0, 0, 0)),
        pl.BlockSpec((nch, 1, hc2), lambda i: (0, 0, 0)),
        pl.BlockSpec((nch, hc2 // 2, d), lambda i: (0, 0, 0), pipeline_mode=pl.Buffered(1)),
    ]
    return pl.pallas_call(
        functools.partial(_ffn_kernel, tm=tm, nblk=nblk),
        grid=(nblk,),
        in_specs=in_specs,
        out_specs=pl.BlockSpec((tm, d), lambda i: (i, 0)),
        out_shape=jax.ShapeDtypeStruct((rows, d), F32),
        scratch_shapes=[
            pltpu.VMEM((tm + 2 * HALO, d), BF16),
            pltpu.VMEM((nch, tm + 2 * HALO, hc2), F32),
            pltpu.VMEM((nch, tm, hc2 // 2), BF16),
        ],
        compiler_params=_cparams(("arbitrary",)),
        name="conv_ffn",
    )(x, x, x, mul, shift, gate, wup, cw_c, cb_c, wdn_c)


def _ffn_chunked(w):
    lead = w.shape[:-1]
    nch = FFN_HIDDEN // FFN_CHUNK
    w2 = w.reshape(lead + (2, nch, FFN_CHUNK))
    w2 = jnp.moveaxis(w2, -2, 0)
    return w2.reshape((nch,) + lead + (2 * FFN_CHUNK,))


def _pad_lanes(v, n):
    return jnp.pad(v, [(0, 0)] * (v.ndim - 1) + [(0, n - v.shape[-1])])


def kernel(x, c, ctx, c_ctx, norm_w, w_mod, b_mod, ssm_w_in, ssm_conv_w, ssm_conv_b, ssm_a_log, ssm_dt_bias, ssm_d_skip, ssm_norm_w, ssm_w_out, sgu_w_in, sgu_ln_w, sgu_ln_b, sgu_w_s, sgu_b_s, sgu_w_out, na_w_qkv, na_q_norm, na_k_norm, na_rpb, na_w_out, ffn_w_up, ffn_conv_w, ffn_conv_b, ffn_w_down):
    assert x.shape[0] == 1 and c.shape[0] == 1
    d = D_MODEL
    depth = w_mod.shape[0]
    x_lat = x[0]
    x_ctx = ctx[0]
    cond = jnp.zeros((8, d), F32).at[0].set(c[0]).at[1].set(c_ctx)
    mods = _mod_vectors(cond, w_mod, b_mod)

    def row(v):
        return v.reshape(1, -1)

    for i in range(depth):
        kind, j = i % N_MIXERS, i // N_MIXERS
        need_ctx = i < depth - 1
        ml = [row(mods[i, 0, k * d:(k + 1) * d]) for k in range(6)]
        mc = [row(mods[i, 1, k * d:(k + 1) * d]) for k in range(6)]
        nw0, nw1 = row(norm_w[i, 0]), row(norm_w[i, 1])
        mul_l, mul_c = nw0 * (1.0 + ml[1]), nw0 * (1.0 + mc[1])

        if kind == 0:
            w_in = ssm_w_in[j]
            w_zxbc = w_in[:, :SSM_ZXBC].astype(BF16)
            w_dt = w_in[:, SSM_ZXBC:]
            w_dt = jnp.concatenate([_pad_lanes(w_dt[:, :SSM_HEADS], LANES),
                                    _pad_lanes(w_dt[:, SSM_HEADS:], LANES)], axis=1).astype(BF16)
            cw = jnp.concatenate([jnp.zeros((SSM_CONV, SSM_INNER), F32), ssm_conv_w[j]], axis=1)
            cb = jnp.concatenate([jnp.zeros((SSM_INNER,), F32), ssm_conv_b[j]]).reshape(1, -1)
            a_neg = _pad_lanes(-jnp.exp(ssm_a_log[j]), LANES)
            dtb = _pad_lanes(ssm_dt_bias[j], LANES)
            dsk = row(jnp.repeat(ssm_d_skip[j, 0] + ssm_d_skip[j, 1], SSM_HEAD_DIM))
            gnw = row(ssm_norm_w[j])
            w_out = ssm_w_out[j].astype(BF16)
            h0 = jnp.zeros((SSM_GROUPS, SSM_STATE, SSM_HPG * SSM_HEAD_DIM), F32)

            def mixer(xs, mul, shift, hf0, hb0):
                zxbc, dt_raw = _ssm_in(xs, mul, shift, w_zxbc, w_dt, cw, cb)
                prep = _ssd_prep(dt_raw, dtb, a_neg)
                yf, hf = _ssd_scan(zxbc, prep, hf0, False)
                gn, hb = _ssd_scan(zxbc, prep, hb0, True, y_other=yf, d_skip=dsk, norm_w=gnw)
                return gn, hf, hb

            gn_c, hf, hb = mixer(x_ctx, mul_c, mc[0], h0, h0)
            gn_l, _, _ = mixer(x_lat, mul_l, ml[0], hf, hb)
            x_lat = _mm_res(gn_l, w_out, x_lat, ml[2])
            if need_ctx:
                x_ctx = _mm_res(gn_c, w_out, x_ctx, mc[2])
        elif kind == 1:
            w_in = sgu_w_in[j].astype(BF16)
            w_s = sgu_w_s[j].astype(BF16)
            gw = SGU_INNER // SGU_GROUPS
            bs_exp = jnp.repeat(sgu_b_s[j].T, gw, axis=1)
            w_out = sgu_w_out[j].astype(BF16)
            lnw, lnb = row(sgu_ln_w[j]), row(sgu_ln_b[j])
            z_l = _modmm(x_lat, mul_l, ml[0], w_in, "gelu")
            x_lat = _sgu_core(z_l, x_lat, lnw, lnb, w_s, bs_exp, w_out, ml[2])
            if need_ctx:
                z_c = _modmm(x_ctx, mul_c, mc[0], w_in, "gelu")
                x_ctx = _sgu_core(z_c, x_ctx, lnw, lnb, w_s, bs_exp, w_out, mc[2])
        else:
            w_qkv = na_w_qkv[j].astype(BF16)
            scale = NA_HEAD_DIM ** -0.5
            nw = jnp.concatenate([jnp.tile(na_q_norm[j] * scale, NA_HEADS),
                                  jnp.tile(na_k_norm[j], NA_HEADS),
                                  jnp.ones((d,), F32)]).reshape(1, -1)
            w_out = na_w_out[j].astype(BF16)
            bias = _na_bias_table(na_rpb[j], x_lat.shape[0] // GRID_W)
            qkv_c = _modmm(x_ctx, mul_c, mc[0], w_qkv, "qkv", nw)
            qkv_l = _modmm(x_lat, mul_l, ml[0], w_qkv, "qkv", nw)
            o_l = _na_attention(qkv_l, qkv_c, bias)
            x_lat = _mm_res(o_l, w_out, x_lat, ml[2])
            if need_ctx:
                o_c = _ctx_attention(qkv_c)
                x_ctx = _mm_res(o_c, w_out, x_ctx, mc[2])

        nch = FFN_HIDDEN // FFN_CHUNK
        wup_c = ffn_w_up[i].astype(BF16)
        cw_c = _ffn_chunked(ffn_conv_w[i])
        cb_c = _ffn_chunked(ffn_conv_b[i].reshape(1, -1))
        wdn_c = ffn_w_down[i].astype(BF16).reshape(nch, FFN_CHUNK, d)
        x_lat = _ffn(x_lat, nw1 * (1.0 + ml[4]), ml[3], ml[5], wup_c, cw_c, cb_c, wdn_c)
        if need_ctx:
            x_ctx = _ffn(x_ctx, nw1 * (1.0 + mc[4]), mc[3], mc[5], wup_c, cw_c, cb_c, wdn_c)
    return x_lat[None]
```

```python
import functools
import math

import numpy as np
import jax
import jax.numpy as jnp
from jax import lax
from jax.experimental import pallas as pl
from jax.experimental.pallas import tpu as pltpu

F32 = jnp.float32
BF16 = jnp.bfloat16
HIGHEST = lax.Precision.HIGHEST

D_MODEL = 1024
DEPTH = 4
N_MIXERS = 3
EPS = 1e-6
GRID_W = 64
SSM_INNER = 2 * D_MODEL
SSM_HEAD_DIM = 64
SSM_HEADS = SSM_INNER // SSM_HEAD_DIM
SSM_GROUPS = 8
SSM_HPG = SSM_HEADS // SSM_GROUPS
SSM_STATE = 128
SSM_CONV = 7
SSD_CHUNK = 128
SSM_GN = SSM_GROUPS * SSM_STATE
SSM_ZXBC = 2 * SSM_INNER + 2 * SSM_GN
SGU_INNER = 2 * D_MODEL
SGU_GROUPS = 8
TOKEN_CHUNK = 128
NA_HEAD_DIM = 64
NA_HEADS = D_MODEL // NA_HEAD_DIM
NA_ROW_WIN = 8
NA_COL_WIN = 16
FFN_HIDDEN = 2816
FFN_CONV = 3

LANES = 128
BF16_SUBLANES = 16
VMEM_LIMIT = 56 * 1024 * 1024

HALO = BF16_SUBLANES
FFN_CHUNK = 256
ROW_PITCH = 2
NA_RB = 4
NA_WR = NA_RB + NA_ROW_WIN


def _cparams(sem, flags=None):
    return pltpu.CompilerParams(dimension_semantics=sem, vmem_limit_bytes=VMEM_LIMIT, flags=flags)


def _row_tile(n, pref):
    t = min(n, pref)
    assert n % t == 0
    return t


def _sigmoid(v):
    return 1.0 / (1.0 + jnp.exp(-v))


def _silu(v):
    return v * _sigmoid(v)


def _modulate(x, mul, shift):
    ms = jnp.mean(x * x, axis=-1, keepdims=True)
    return x * lax.rsqrt(ms + EPS) * mul + shift


def _lane_lt(shape, n):
    return lax.broadcasted_iota(jnp.int32, shape, len(shape) - 1) < n


def _mod_kernel(c_ref, w_ref, b_ref, o_ref):
    s = _silu(c_ref[...])
    o_ref[0] = jnp.dot(s, w_ref[0], preferred_element_type=F32, precision=HIGHEST) + b_ref[0]


def _mod_vectors(cond, w_mod, b_mod):
    depth, d, n = w_mod.shape
    tn = 1536
    return pl.pallas_call(
        _mod_kernel,
        grid=(depth, n // tn),
        in_specs=[
            pl.BlockSpec((8, d), lambda i, j: (0, 0)),
            pl.BlockSpec((1, d, tn), lambda i, j: (i, 0, j)),
            pl.BlockSpec((1, 1, tn), lambda i, j: (i, 0, j)),
        ],
        out_specs=pl.BlockSpec((1, 8, tn), lambda i, j: (i, 0, j)),
        out_shape=jax.ShapeDtypeStruct((depth, 8, n), F32),
        compiler_params=_cparams(("arbitrary", "arbitrary")),
        name="mod_vectors",
    )(cond, w_mod, b_mod.reshape(depth, 1, n))


def _gelu_tanh(v):
    c = math.sqrt(2.0 / math.pi)
    return v * (0.5 * (1.0 + jnp.tanh(c * (v + 0.044715 * (v * v * v)))))


def _head_rmsnorm(blk, nw):
    lo = _lane_lt(blk.shape, NA_HEAD_DIM)
    sq = blk * blk
    s_lo = jnp.sum(jnp.where(lo, sq, 0.0), axis=-1, keepdims=True)
    s_hi = jnp.sum(jnp.where(lo, 0.0, sq), axis=-1, keepdims=True)
    r_lo = lax.rsqrt(s_lo * (1.0 / NA_HEAD_DIM) + EPS)
    r_hi = lax.rsqrt(s_hi * (1.0 / NA_HEAD_DIM) + EPS)
    return blk * jnp.where(lo, r_lo, r_hi) * nw


def _modmm_kernel(x_ref, mul_ref, sh_ref, w_ref, *rest, mode, tn):
    if mode == "qkv":
        nw_ref, o_ref, h_scr = rest
    else:
        o_ref, h_scr = rest
    h_scr[...] = _modulate(x_ref[...], mul_ref[...], sh_ref[...]).astype(BF16)
    n = w_ref.shape[1]
    for j in range(n // tn):
        col = slice(j * tn, (j + 1) * tn)
        acc = jnp.dot(h_scr[...], w_ref[:, col], preferred_element_type=F32)
        if mode == "gelu":
            o_ref[:, col] = _gelu_tanh(acc).astype(o_ref.dtype)
        elif j < 2 * D_MODEL // tn:
            for b in range(tn // LANES):
                sl = slice(j * tn + b * LANES, j * tn + (b + 1) * LANES)
                o_ref[:, sl] = _head_rmsnorm(acc[:, b * LANES:(b + 1) * LANES], nw_ref[:, sl]).astype(o_ref.dtype)
        else:
            o_ref[:, col] = acc.astype(o_ref.dtype)


def _modmm(x, mul, shift, w, mode, nw=None):
    rows, d = x.shape
    n = w.shape[1]
    tm = _row_tile(rows, 512)
    tn = 256
    in_specs = [
        pl.BlockSpec((tm, d), lambda i: (i, 0)),
        pl.BlockSpec((1, d), lambda i: (0, 0)),
        pl.BlockSpec((1, d), lambda i: (0, 0)),
        pl.BlockSpec((d, n), lambda i: (0, 0), pipeline_mode=pl.Buffered(1)),
    ]
    args = [x, mul, shift, w]
    if mode == "qkv":
        in_specs.append(pl.BlockSpec((1, n), lambda i: (0, 0)))
        args.append(nw)
    return pl.pallas_call(
        functools.partial(_modmm_kernel, mode=mode, tn=tn),
        grid=(rows // tm,),
        in_specs=in_specs,
        out_specs=pl.BlockSpec((tm, n), lambda i: (i, 0)),
        out_shape=jax.ShapeDtypeStruct((rows, n), BF16),
        scratch_shapes=[pltpu.VMEM((tm, d), BF16)],
        compiler_params=_cparams(("arbitrary",)),
        name="modmm_" + mode,
    )(*args)


def _mmres_kernel(a_ref, w_ref, x_ref, g_ref, o_ref):
    y = jnp.dot(a_ref[...], w_ref[...], preferred_element_type=F32)
    o_ref[...] = x_ref[...] + g_ref[...] * y


def _mm_res(a, w, x, gate):
    rows, k = a.shape
    d = w.shape[1]
    tm = _row_tile(rows, 1024)
    return pl.pallas_call(
        _mmres_kernel,
        grid=(rows // tm,),
        in_specs=[
            pl.BlockSpec((tm, k), lambda i: (i, 0)),
            pl.BlockSpec((k, d), lambda i: (0, 0)),
            pl.BlockSpec((tm, d), lambda i: (i, 0)),
            pl.BlockSpec((1, d), lambda i: (0, 0)),
        ],
        out_specs=pl.BlockSpec((tm, d), lambda i: (i, 0)),
        out_shape=jax.ShapeDtypeStruct((rows, d), F32),
        compiler_params=_cparams(("arbitrary",)),
        name="mm_res",
    )(a, w, x, gate)


def _fill_h_with_halo(h_scr, xp_ref, x_ref, xn_ref, mul, sh, tm, nblk):
    i = pl.program_id(0)
    h_scr[HALO:HALO + tm, :] = _modulate(x_ref[...], mul, sh).astype(BF16)
    hp = jnp.where(i > 0, _modulate(xp_ref[...], mul, sh), 0.0)
    hn = jnp.where(i < nblk - 1, _modulate(xn_ref[...], mul, sh), 0.0)
    h_scr[0:HALO, :] = hp.astype(BF16)
    h_scr[HALO + tm:HALO + tm + HALO, :] = hn.astype(BF16)


def _halo_specs(tm, d, rows):
    per = tm // HALO
    last = rows // HALO - 1
    return [
        pl.BlockSpec((HALO, d), lambda i, *_: (jnp.maximum(i * per - 1, 0), 0)),
        pl.BlockSpec((tm, d), lambda i, *_: (i, 0)),
        pl.BlockSpec((HALO, d), lambda i, *_: (jnp.minimum((i + 1) * per, last), 0)),
    ]


def _dwconv_from_scratch(acc_scr, cw_ref, cb_ref, taps, tm):
    base = HALO - taps // 2
    y = cb_ref[...] + cw_ref[0:1, :] * acc_scr[pl.ds(base, tm), :]
    for k in range(1, taps):
        y = y + cw_ref[k:k + 1, :] * acc_scr[pl.ds(base + k, tm), :]
    return y


def _dwconv_pitched(a, acc_scr, slab0, cw_ref, cb_ref, col0, taps, tm, finish):
    n, c = a.shape
    for l in range(c // LANES):
        acc_scr[slab0 + l, pl.ds(0, n, stride=ROW_PITCH), :] = a[:, l * LANES:(l + 1) * LANES]
    for l in range(c // LANES):
        lanes = slice(col0 + l * LANES, col0 + (l + 1) * LANES)
        y = cb_ref[:, lanes]
        for k in range(taps):
            off = HALO - taps // 2 + k
            y = y + cw_ref[k:k + 1, lanes] * acc_scr[slab0 + l, pl.ds(ROW_PITCH * off, tm, stride=ROW_PITCH), :]
        finish(lanes, y)


def _ssm_in_kernel(xp_ref, x_ref, xn_ref, mul_ref, sh_ref, w_ref, wdt_ref, cw_ref, cb_ref,
                   o_ref, dt_ref, h_scr, acc_scr, *, tm, nblk, tn):
    _fill_h_with_halo(h_scr, xp_ref, x_ref, xn_ref, mul_ref[...], sh_ref[...], tm, nblk)
    h_mid = h_scr[HALO:HALO + tm, :]
    dt_ref[...] = jnp.dot(h_mid, wdt_ref[...], preferred_element_type=F32)
    for j in range(SSM_INNER // tn):
        col = slice(j * tn, (j + 1) * tn)
        o_ref[:, col] = jnp.dot(h_mid, w_ref[:, col], preferred_element_type=F32).astype(o_ref.dtype)
    def finish(lanes, y):
        o_ref[:, lanes] = _silu(y).astype(o_ref.dtype)

    j0 = SSM_INNER // tn
    for j in range(j0, SSM_ZXBC // tn):
        col = slice(j * tn, (j + 1) * tn)
        a = jnp.dot(h_scr[...], w_ref[:, col], preferred_element_type=F32)
        _dwconv_pitched(a, acc_scr, (j - j0) * (tn // LANES), cw_ref, cb_ref, j * tn, SSM_CONV, tm, finish)


def _ssm_in(x, mul, shift, w_zxbc, w_dt, conv_w, conv_b):
    rows, d = x.shape
    n = w_zxbc.shape[1]
    tm = _row_tile(rows, 512)
    tn = 256
    nblk = rows // tm
    ndt = w_dt.shape[1]
    in_specs = _halo_specs(tm, d, rows) + [
        pl.BlockSpec((1, d), lambda i: (0, 0)),
        pl.BlockSpec((1, d), lambda i: (0, 0)),
        pl.BlockSpec((d, n), lambda i: (0, 0), pipeline_mode=pl.Buffered(1)),
        pl.BlockSpec((d, ndt), lambda i: (0, 0)),
        pl.BlockSpec((SSM_CONV, n), lambda i: (0, 0)),
        pl.BlockSpec((1, n), lambda i: (0, 0)),
    ]
    return pl.pallas_call(
        functools.partial(_ssm_in_kernel, tm=tm, nblk=nblk, tn=tn),
        grid=(nblk,),
        in_specs=in_specs,
        out_specs=[
            pl.BlockSpec((tm, n), lambda i: (i, 0)),
            pl.BlockSpec((tm, ndt), lambda i: (i, 0)),
        ],
        out_shape=[
            jax.ShapeDtypeStruct((rows, n), BF16),
            jax.ShapeDtypeStruct((rows, ndt), F32),
        ],
        scratch_shapes=[
            pltpu.VMEM((tm + 2 * HALO, d), BF16),
            pltpu.VMEM(((n - SSM_INNER) // LANES, ROW_PITCH * (tm + 2 * HALO), LANES), F32),
        ],
        compiler_params=_cparams(("arbitrary",)),
        name="ssm_in",
    )(x, x, x, mul, shift, w_zxbc, w_dt, conv_w, conv_b)


def _softplus(v):
    return jnp.maximum(v, 0.0) + jnp.log1p(jnp.exp(-jnp.abs(v)))


LOG2E = 1.4426950408889634


def _scan_mask(q, rev):
    row = lax.broadcasted_iota(jnp.int32, (q, q), 0)
    col = lax.broadcasted_iota(jnp.int32, (q, q), 1)
    return (col >= row) if rev else (row >= col)


def _hi_lo(v):
    hi = v.astype(BF16)
    return hi, (v - hi.astype(F32)).astype(BF16)


def _ssd_prep_kernel(dt_ref, dtb_ref, an_ref, cum2_ref, rowt_ref, ef_ref, etot_ref, *, nck):
    q = SSD_CHUNK
    for d in range(2):
        rev = d == 1
        tri = _scan_mask(q, rev).astype(F32)
        for ck in range(nck):
            rs = slice(ck * q, (ck + 1) * q)
            dt = _softplus(dt_ref[rs, d * LANES:(d + 1) * LANES] + dtb_ref[d:d + 1, :])
            d_a = dt * an_ref[d:d + 1, :]
            cum = jnp.dot(tri, d_a, preferred_element_type=F32, precision=HIGHEST)
            tot = cum[0:1, :] if rev else cum[q - 1:q, :]
            cum2_ref[d, rs, :] = cum * LOG2E
            rowt_ref[d, ck] = ((cum - jnp.log(dt)) * LOG2E).T
            parts = _hi_lo(jnp.exp(cum)) + _hi_lo(dt * jnp.exp(tot - cum))
            for k, part in enumerate(parts):
                ef_ref[d, rs, k * LANES:(k + 1) * LANES] = part
            etot_ref[d, ck] = jnp.broadcast_to(jnp.exp(tot), (8, LANES))


def _ssd_prep(dt_raw, dt_bias, a_neg):
    rows = dt_raw.shape[0]
    q = SSD_CHUNK
    tm = _row_tile(rows, 1024)
    nck = tm // q
    nc = rows // q
    col_spec = pl.BlockSpec((2, tm, LANES), lambda i: (0, i, 0))
    return pl.pallas_call(
        functools.partial(_ssd_prep_kernel, nck=nck),
        grid=(rows // tm,),
        in_specs=[
            pl.BlockSpec((tm, 2 * LANES), lambda i: (i, 0)),
            pl.BlockSpec((2, LANES), lambda i: (0, 0)),
            pl.BlockSpec((2, LANES), lambda i: (0, 0)),
        ],
        out_specs=[
            col_spec,
            pl.BlockSpec((2, nck, LANES, q), lambda i: (0, i, 0, 0)),
            pl.BlockSpec((2, tm, 4 * LANES), lambda i: (0, i, 0)),
            pl.BlockSpec((2, nck, 8, LANES), lambda i: (0, i, 0, 0)),
        ],
        out_shape=[
            jax.ShapeDtypeStruct((2, rows, LANES), F32),
            jax.ShapeDtypeStruct((2, nc, LANES, q), F32),
            jax.ShapeDtypeStruct((2, rows, 4 * LANES), BF16),
            jax.ShapeDtypeStruct((2, nc, 8, LANES), F32),
        ],
        compiler_params=_cparams(("arbitrary",)),
        name="ssd_prep",
    )(dt_raw, dt_bias, a_neg)


def _ssd_kernel(*refs, rev, epilogue):
    if epilogue:
        (x_ref, b_ref, c_ref, cum2_ref, rowt_ref, ef_ref, etot_ref, sel_ref, h0_ref,
         z_ref, yf_ref, dsk_ref, nw_ref, y_ref, hf_ref, h_scr) = refs
    else:
        (x_ref, b_ref, c_ref, cum2_ref, rowt_ref, ef_ref, etot_ref, sel_ref, h0_ref,
         y_ref, hf_ref, h_scr) = refs
    q = SSD_CHUNK
    hd = SSM_HEAD_DIM
    step = pl.program_id(0)

    @pl.when(step == 0)
    def _():
        h_scr[...] = h0_ref[...]

    tmask = _scan_mask(q, rev)
    cum2 = cum2_ref[0]
    rowt = rowt_ref[0, 0]
    etot = etot_ref[0, 0][0:1, :]
    lo = _lane_lt((q, LANES), hd)
    lo1 = _lane_lt((1, LANES), hd)
    gw = SSM_HPG * hd

    for g in range(SSM_GROUPS):
        bg = b_ref[:, g * SSM_STATE:(g + 1) * SSM_STATE]
        cg = c_ref[:, g * SSM_STATE:(g + 1) * SSM_STATE]
        cb = lax.dot_general(cg, bg, (((1,), (1,)), ((), ())), preferred_element_type=F32)
        h_t = h_scr[g]
        esc_g = jnp.dot(ef_ref[0, :, 0:2 * LANES], sel_ref[g], preferred_element_type=F32)
        fsc_g = jnp.dot(ef_ref[0, :, 2 * LANES:4 * LANES], sel_ref[g], preferred_element_type=F32)
        yoff = jnp.dot(cg, h_t.astype(BF16), preferred_element_type=F32) * esc_g
        wg = (x_ref[:, g * gw:(g + 1) * gw].astype(F32) * fsc_g).astype(BF16)
        dec_rows = []
        gated = []
        for pair in range(SSM_HPG // 2):
            ms = []
            for rr in range(2):
                h = g * SSM_HPG + 2 * pair + rr
                seg = cum2[:, h:h + 1] - rowt[h:h + 1, :]
                ms.append((cb * jnp.exp2(jnp.where(tmask, seg, -jnp.inf))).astype(BF16))
            h0i = g * SSM_HPG + 2 * pair
            c0 = g * SSM_HPG * hd + pair * LANES
            xp = x_ref[:, c0:c0 + LANES]
            zero = jnp.zeros_like(xp)
            xcat = jnp.concatenate([jnp.where(lo, xp, zero), jnp.where(lo, zero, xp)], axis=0)
            ydiag = jnp.dot(jnp.concatenate(ms, axis=1), xcat, preferred_element_type=F32)
            y_pair = ydiag + yoff[:, pair * LANES:(pair + 1) * LANES]
            dec_rows.append(jnp.where(lo1, etot[:, h0i:h0i + 1], etot[:, h0i + 1:h0i + 2]))
            if epilogue:
                yt = y_pair + yf_ref[:, c0:c0 + LANES].astype(F32) + dsk_ref[:, c0:c0 + LANES] * xp.astype(F32)
                gated.append(yt * _silu(z_ref[:, c0:c0 + LANES].astype(F32)))
            else:
                y_ref[:, c0:c0 + LANES] = y_pair.astype(y_ref.dtype)
        if epilogue:
            ssq = sum(jnp.sum(v * v, axis=-1, keepdims=True) for v in gated)
            rinv = lax.rsqrt(ssq * (1.0 / (SSM_HPG * hd)) + EPS)
            for pair, v in enumerate(gated):
                c0 = g * SSM_HPG * hd + pair * LANES
                y_ref[:, c0:c0 + LANES] = (v * rinv * nw_ref[:, c0:c0 + LANES]).astype(y_ref.dtype)
        st = lax.dot_general(bg, wg, (((0,), (0,)), ((), ())), preferred_element_type=F32)
        h_scr[g] = h_t * jnp.concatenate(dec_rows, axis=1) + st

    @pl.when(step == pl.num_programs(0) - 1)
    def _():
        hf_ref[...] = h_scr[...]


def _head_select():
    head = np.arange(LANES)[None, :, None]
    col = np.arange(SSM_HPG * SSM_HEAD_DIM)[None, None, :]
    g = np.arange(SSM_GROUPS)[:, None, None]
    one = (head == g * SSM_HPG + col // SSM_HEAD_DIM).astype(np.float32)
    return jnp.asarray(np.concatenate([one, one], axis=1), dtype=BF16)


def _ssd_scan(zxbc, prep, h0, rev, y_other=None, d_skip=None, norm_w=None):
    rows = zxbc.shape[0]
    q = SSD_CHUNK
    nc = rows // q
    epilogue = y_other is not None
    d = 1 if rev else 0

    def cidx(s):
        return (nc - 1 - s) if rev else s

    col_spec = pl.BlockSpec((1, q, LANES), lambda s: (d, cidx(s), 0))
    in_specs = [
        pl.BlockSpec((q, SSM_INNER), lambda s: (cidx(s), 1)),
        pl.BlockSpec((q, SSM_GN), lambda s: (cidx(s), 2 * SSM_INNER // SSM_GN)),
        pl.BlockSpec((q, SSM_GN), lambda s: (cidx(s), 2 * SSM_INNER // SSM_GN + 1)),
        col_spec,
        pl.BlockSpec((1, 1, LANES, q), lambda s: (d, cidx(s), 0, 0)),
        pl.BlockSpec((1, q, 4 * LANES), lambda s: (d, cidx(s), 0)),
        pl.BlockSpec((1, 1, 8, LANES), lambda s: (d, cidx(s), 0, 0)),
        pl.BlockSpec((SSM_GROUPS, 2 * LANES, SSM_HPG * SSM_HEAD_DIM), lambda s: (0, 0, 0)),
        pl.BlockSpec((SSM_GROUPS, SSM_STATE, SSM_HPG * SSM_HEAD_DIM), lambda s: (0, 0, 0)),
    ]
    cum2, rowt, ef, etot = prep
    args = [zxbc, zxbc, zxbc, cum2, rowt, ef, etot, _head_select(), h0]
    if epilogue:
        in_specs += [
            pl.BlockSpec((q, SSM_INNER), lambda s: (cidx(s), 0)),
            pl.BlockSpec((q, SSM_INNER), lambda s: (cidx(s), 0)),
            pl.BlockSpec((1, SSM_INNER), lambda s: (0, 0)),
            pl.BlockSpec((1, SSM_INNER), lambda s: (0, 0)),
        ]
        args += [zxbc, y_other, d_skip, norm_w]
    state_shape = (SSM_GROUPS, SSM_STATE, SSM_HPG * SSM_HEAD_DIM)
    return pl.pallas_call(
        functools.partial(_ssd_kernel, rev=rev, epilogue=epilogue),
        grid=(nc,),
        in_specs=in_specs,
        out_specs=[
            pl.BlockSpec((q, SSM_INNER), lambda s: (cidx(s), 0)),
            pl.BlockSpec(state_shape, lambda s: (0, 0, 0)),
        ],
        out_shape=[
            jax.ShapeDtypeStruct((rows, SSM_INNER), BF16),
            jax.ShapeDtypeStruct(state_shape, F32),
        ],
        scratch_shapes=[pltpu.VMEM(state_shape, F32)],
        compiler_params=_cparams(("arbitrary",)),
        name="ssd_bwd" if rev else "ssd_fwd",
    )(*args)


def _sgu_kernel(u_ref, v_ref, x_ref, lnw_ref, lnb_ref, ws_ref, bs_ref, wo_ref, g_ref, o_ref,
                vn_scr, uv_scr, *, tm):
    v = v_ref[...].astype(F32)
    mu = jnp.mean(v, axis=-1, keepdims=True)
    vc = v - mu
    var = jnp.mean(vc * vc, axis=-1, keepdims=True)
    vn_scr[...] = (vc * lax.rsqrt(var + EPS) * lnw_ref[...] + lnb_ref[...]).astype(BF16)
    gw = SGU_INNER // SGU_GROUPS
    for ch in range(tm // TOKEN_CHUNK):
        rs = slice(ch * TOKEN_CHUNK, (ch + 1) * TOKEN_CHUNK)
        for g in range(SGU_GROUPS):
            cs = slice(g * gw, (g + 1) * gw)
            sv = jnp.dot(ws_ref[g], vn_scr[rs, cs], preferred_element_type=F32) + bs_ref[:, cs]
            uv_scr[rs, cs] = (u_ref[rs, cs].astype(F32) * sv).astype(BF16)
    y = jnp.dot(uv_scr[...], wo_ref[...], preferred_element_type=F32)
    o_ref[...] = x_ref[...] + g_ref[...] * y


def _sgu_core(z, x, ln_w, ln_b, w_s, bs_exp, w_out, gate):
    rows, d = x.shape
    e = SGU_INNER
    tm = _row_tile(rows, 512)
    return pl.pallas_call(
        functools.partial(_sgu_kernel, tm=tm),
        grid=(rows // tm,),
        in_specs=[
            pl.BlockSpec((tm, e), lambda i: (i, 0)),
            pl.BlockSpec((tm, e), lambda i: (i, 1)),
            pl.BlockSpec((tm, d), lambda i: (i, 0)),
            pl.BlockSpec((1, e), lambda i: (0, 0)),
            pl.BlockSpec((1, e), lambda i: (0, 0)),
            pl.BlockSpec((SGU_GROUPS, TOKEN_CHUNK, TOKEN_CHUNK), lambda i: (0, 0, 0)),
            pl.BlockSpec((TOKEN_CHUNK, e), lambda i: (0, 0)),
            pl.BlockSpec((e, d), lambda i: (0, 0)),
            pl.BlockSpec((1, d), lambda i: (0, 0)),
        ],
        out_specs=pl.BlockSpec((tm, d), lambda i: (i, 0)),
        out_shape=jax.ShapeDtypeStruct((rows, d), F32),
        scratch_shapes=[pltpu.VMEM((tm, e), BF16), pltpu.VMEM((tm, e), BF16)],
        compiler_params=_cparams(("arbitrary",)),
        name="sgu_core",
    )(z, z, x, ln_w, ln_b, w_s, bs_exp, w_out, gate)


def _softmax_pv(s_list, v_list):
    m = s_list[0].max(axis=-1, keepdims=True)
    for s in s_list[1:]:
        m = jnp.maximum(m, s.max(axis=-1, keepdims=True))
    ps = [jnp.exp(s - m) for s in s_list]
    den = sum(p.sum(axis=-1, keepdims=True) for p in ps)
    o = sum(jnp.dot(p.astype(BF16), v, preferred_element_type=F32) for p, v in zip(ps, v_list))
    return o * (1.0 / den)


_NT = (((1,), (1,)), ((), ()))


def _na_kernel(q_ref, k_ref, v_ref, kc_ref, vc_ref, bias_ref, o_ref, *, rows, nblk, nsub):
    kc = kc_ref[...]
    vc = vc_ref[...]
    tq = NA_RB * GRID_W
    lo = _lane_lt((tq, LANES), NA_HEAD_DIM)
    for sb in range(nsub):
        rb = pl.program_id(1) * nsub + sb
        wstart = jnp.clip(rb * NA_RB - NA_ROW_WIN // 2, 0, rows - NA_WR)
        variant = jnp.where(rb == 0, 0, jnp.where(rb == nblk - 1, 2, 1))
        koff = pl.multiple_of(wstart * GRID_W, GRID_W)
        kw = k_ref[pl.ds(koff, NA_WR * GRID_W), :]
        vw = v_ref[pl.ds(koff, NA_WR * GRID_W), :]
        qv = q_ref[sb * tq:(sb + 1) * tq, :]
        zero = jnp.zeros_like(qv)
        outs = []
        for hh in range(2):
            qm = jnp.where(lo, qv, zero) if hh == 0 else jnp.where(lo, zero, qv)
            s_win = lax.dot_general(qm, kw, _NT, preferred_element_type=F32) + bias_ref[hh, variant]
            s_ctx = lax.dot_general(qm, kc, _NT, preferred_element_type=F32)
            outs.append(_softmax_pv([s_win, s_ctx], [vw, vc]))
        o_ref[sb * tq:(sb + 1) * tq, :] = jnp.where(lo, outs[0], outs[1]).astype(o_ref.dtype)


def _na_attention(qkv_lat, qkv_ctx, bias):
    n_lat = qkv_lat.shape[0]
    n_ctx = qkv_ctx.shape[0]
    rows = n_lat // GRID_W
    nblk = rows // NA_RB
    assert rows >= NA_WR and rows % NA_RB == 0
    hp_n = NA_HEADS // 2
    nsub = 4 if nblk % 4 == 0 else 1
    tq = nsub * NA_RB * GRID_W
    return pl.pallas_call(
        functools.partial(_na_kernel, rows=rows, nblk=nblk, nsub=nsub),
        grid=(hp_n, nblk // nsub),
        in_specs=[
            pl.BlockSpec((tq, LANES), lambda hp, rb: (rb, hp)),
            pl.BlockSpec((n_lat, LANES), lambda hp, rb: (0, hp_n + hp)),
            pl.BlockSpec((n_lat, LANES), lambda hp, rb: (0, 2 * hp_n + hp)),
            pl.BlockSpec((n_ctx, LANES), lambda hp, rb: (0, hp_n + hp)),
            pl.BlockSpec((n_ctx, LANES), lambda hp, rb: (0, 2 * hp_n + hp)),
            pl.BlockSpec((2, 3, NA_RB * GRID_W, NA_WR * GRID_W), lambda hp, rb: (hp, 0, 0, 0)),
        ],
        out_specs=pl.BlockSpec((tq, LANES), lambda hp, rb: (rb, hp)),
        out_shape=jax.ShapeDtypeStruct((n_lat, D_MODEL), BF16),
        compiler_params=_cparams(("arbitrary", "arbitrary")),
        name="na_attention",
    )(qkv_lat, qkv_lat, qkv_lat, qkv_ctx, qkv_ctx, bias)


def _ctx_attn_kernel(q_ref, k_ref, v_ref, o_ref):
    qv = q_ref[...]
    lo = _lane_lt(qv.shape, NA_HEAD_DIM)
    zero = jnp.zeros_like(qv)
    outs = []
    for hh in range(2):
        qm = jnp.where(lo, qv, zero) if hh == 0 else jnp.where(lo, zero, qv)
        s = lax.dot_general(qm, k_ref[...], _NT, preferred_element_type=F32)
        outs.append(_softmax_pv([s], [v_ref[...]]))
    o_ref[...] = jnp.where(lo, outs[0], outs[1]).astype(o_ref.dtype)


def _ctx_attention(qkv_ctx):
    n_ctx = qkv_ctx.shape[0]
    hp_n = NA_HEADS // 2
    return pl.pallas_call(
        _ctx_attn_kernel,
        grid=(hp_n,),
        in_specs=[
            pl.BlockSpec((n_ctx, LANES), lambda hp: (0, hp)),
            pl.BlockSpec((n_ctx, LANES), lambda hp: (0, hp_n + hp)),
            pl.BlockSpec((n_ctx, LANES), lambda hp: (0, 2 * hp_n + hp)),
        ],
        out_specs=pl.BlockSpec((n_ctx, LANES), lambda hp: (0, hp)),
        out_shape=jax.ShapeDtypeStruct((n_ctx, D_MODEL), BF16),
        compiler_params=_cparams(("arbitrary",)),
        name="ctx_attention",
    )(qkv_ctx, qkv_ctx, qkv_ctx)


def _na_bias_table(rpb, rows):
    col = np.arange(GRID_W)
    col_start = np.clip(col - NA_COL_WIN // 2, 0, GRID_W - NA_COL_WIN)
    in_win = (col[None, :] >= col_start[:, None]) & (col[None, :] < col_start[:, None] + NA_COL_WIN)
    w = GRID_W
    edge = w - NA_COL_WIN
    rp = jnp.pad(rpb.astype(F32), ((0, 0), (0, 0), (edge, edge)))
    col_bias = jnp.stack([rp[:, :, w - 1 - qc:2 * w - 1 - qc] for qc in range(w)], axis=2)
    col_bias = jnp.where(in_win, col_bias, -jnp.inf)
    cbt = jnp.transpose(col_bias, (0, 2, 1, 3)).reshape(NA_HEADS, w, (2 * NA_ROW_WIN - 1) * w)
    wr = NA_ROW_WIN
    assert rows >= NA_WR
    variants = []
    for r0 in (0, NA_RB, rows - NA_RB):
        wstart = int(np.clip(r0 - NA_ROW_WIN // 2, 0, rows - NA_WR))
        blocks = []
        for qr in range(NA_RB):
            r = r0 + qr
            rs = int(np.clip(r - wr // 2, 0, rows - wr))
            first = rs - wstart
            a0 = rs - r + NA_ROW_WIN - 1
            blk = cbt[:, :, a0 * w:(a0 + wr) * w]
            blocks.append(jnp.pad(blk, ((0, 0), (0, 0), (first * w, (NA_WR - first - wr) * w)),
                                  constant_values=-jnp.inf))
        variants.append(jnp.concatenate(blocks, axis=1))
    return jnp.stack(variants, axis=1)


def _ffn_kernel(xp_ref, x_ref, xn_ref, mul_ref, sh_ref, gate_ref, wup_ref, cw_ref, cb_ref, wdn_ref,
                o_ref, h_scr, acc_scr, act_scr, *, tm, nblk):
    _fill_h_with_halo(h_scr, xp_ref, x_ref, xn_ref, mul_ref[...], sh_ref[...], tm, nblk)
    hc = FFN_CHUNK
    nch = FFN_HIDDEN // hc
    def up(c):
        for half in range(2):
            col = slice(half * FFN_HIDDEN + c * hc, half * FFN_HIDDEN + (c + 1) * hc)
            acc_scr[c, :, half * hc:(half + 1) * hc] = jnp.dot(
                h_scr[...], wup_ref[:, col], preferred_element_type=F32)

    lead = 2
    for c in range(lead):
        up(c)
    for c in range(nch):
        if c + lead < nch:
            up(c + lead)
        a = _dwconv_from_scratch(acc_scr.at[c], cw_ref.at[c], cb_ref.at[c], FFN_CONV, tm)
        act_scr[c] = (_silu(a[:, :hc]) * a[:, hc:]).astype(BF16)
    y = jnp.dot(act_scr[0], wdn_ref[0], preferred_element_type=F32)
    for c in range(1, nch):
        y = y + jnp.dot(act_scr[c], wdn_ref[c], preferred_element_type=F32)
    o_ref[...] = x_ref[...] + gate_ref[...] * y


def _ffn(x, mul, shift, gate, wup, cw_c, cb_c, wdn_c):
    rows, d = x.shape
    nch, _, hc2 = cw_c.shape
    tm = _row_tile(rows, 512)
    nblk = rows // tm
    in_specs = _halo_specs(tm, d, rows) + [
        pl.BlockSpec((1, d), lambda i: (0, 0)),
        pl.BlockSpec((1, d), lambda i: (0, 0)),
        pl.BlockSpec((1, d), lambda i: (0, 0)),
        pl.BlockSpec((d, 2 * FFN_HIDDEN), lambda i: (0, 0), pipeline_mode=pl.Buffered(1)),
        pl.BlockSpec((nch, FFN_CONV, hc2), lambda i: (0, 0, 0)),
        pl.BlockSpec((nch, 1, hc2), lambda i: (0, 0, 0)),
        pl.BlockSpec((nch, hc2 // 2, d), lambda i: (0, 0, 0), pipeline_mode=pl.Buffered(1)),
    ]
    return pl.pallas_call(
        functools.partial(_ffn_kernel, tm=tm, nblk=nblk),
        grid=(nblk,),
        in_specs=in_specs,
        out_specs=pl.BlockSpec((tm, d), lambda i: (i, 0)),
        out_shape=jax.ShapeDtypeStruct((rows, d), F32),
        scratch_shapes=[
            pltpu.VMEM((tm + 2 * HALO, d), BF16),
            pltpu.VMEM((nch, tm + 2 * HALO, hc2), F32),
            pltpu.VMEM((nch, tm, hc2 // 2), BF16),
        ],
        compiler_params=_cparams(("arbitrary",)),
        name="conv_ffn",
    )(x, x, x, mul, shift, gate, wup, cw_c, cb_c, wdn_c)


def _ffn_chunked(w):
    lead = w.shape[:-1]
    nch = FFN_HIDDEN // FFN_CHUNK
    w2 = w.reshape(lead + (2, nch, FFN_CHUNK))
    w2 = jnp.moveaxis(w2, -2, 0)
    return w2.reshape((nch,) + lead + (2 * FFN_CHUNK,))


def _pad_lanes(v, n):
    return jnp.pad(v, [(0, 0)] * (v.ndim - 1) + [(0, n - v.shape[-1])])


def kernel(x, c, ctx, c_ctx, norm_w, w_mod, b_mod, ssm_w_in, ssm_conv_w, ssm_conv_b, ssm_a_log, ssm_dt_bias, ssm_d_skip, ssm_norm_w, ssm_w_out, sgu_w_in, sgu_ln_w, sgu_ln_b, sgu_w_s, sgu_b_s, sgu_w_out, na_w_qkv, na_q_norm, na_k_norm, na_rpb, na_w_out, ffn_w_up, ffn_conv_w, ffn_conv_b, ffn_w_down):
    assert x.shape[0] == 1 and c.shape[0] == 1
    d = D_MODEL
    depth = w_mod.shape[0]
    x_lat = x[0]
    x_ctx = ctx[0]
    cond = jnp.zeros((8, d), F32).at[0].set(c[0]).at[1].set(c_ctx)
    mods = _mod_vectors(cond, w_mod, b_mod)

    def row(v):
        return v.reshape(1, -1)

    for i in range(depth):
        kind, j = i % N_MIXERS, i // N_MIXERS
        need_ctx = i < depth - 1
        ml = [row(mods[i, 0, k * d:(k + 1) * d]) for k in range(6)]
        mc = [row(mods[i, 1, k * d:(k + 1) * d]) for k in range(6)]
        nw0, nw1 = row(norm_w[i, 0]), row(norm_w[i, 1])
        mul_l, mul_c = nw0 * (1.0 + ml[1]), nw0 * (1.0 + mc[1])

        if kind == 0:
            w_in = ssm_w_in[j]
            w_zxbc = w_in[:, :SSM_ZXBC].astype(BF16)
            w_dt = w_in[:, SSM_ZXBC:]
            w_dt = jnp.concatenate([_pad_lanes(w_dt[:, :SSM_HEADS], LANES),
                                    _pad_lanes(w_dt[:, SSM_HEADS:], LANES)], axis=1).astype(BF16)
            cw = jnp.concatenate([jnp.zeros((SSM_CONV, SSM_INNER), F32), ssm_conv_w[j]], axis=1)
            cb = jnp.concatenate([jnp.zeros((SSM_INNER,), F32), ssm_conv_b[j]]).reshape(1, -1)
            a_neg = _pad_lanes(-jnp.exp(ssm_a_log[j]), LANES)
            dtb = _pad_lanes(ssm_dt_bias[j], LANES)
            dsk = row(jnp.repeat(ssm_d_skip[j, 0] + ssm_d_skip[j, 1], SSM_HEAD_DIM))
            gnw = row(ssm_norm_w[j])
            w_out = ssm_w_out[j].astype(BF16)
            h0 = jnp.zeros((SSM_GROUPS, SSM_STATE, SSM_HPG * SSM_HEAD_DIM), F32)

            def mixer(xs, mul, shift, hf0, hb0):
                zxbc, dt_raw = _ssm_in(xs, mul, shift, w_zxbc, w_dt, cw, cb)
                prep = _ssd_prep(dt_raw, dtb, a_neg)
                yf, hf = _ssd_scan(zxbc, prep, hf0, False)
                gn, hb = _ssd_scan(zxbc, prep, hb0, True, y_other=yf, d_skip=dsk, norm_w=gnw)
                return gn, hf, hb

            gn_c, hf, hb = mixer(x_ctx, mul_c, mc[0], h0, h0)
            gn_l, _, _ = mixer(x_lat, mul_l, ml[0], hf, hb)
            x_lat = _mm_res(gn_l, w_out, x_lat, ml[2])
            if need_ctx:
                x_ctx = _mm_res(gn_c, w_out, x_ctx, mc[2])
        elif kind == 1:
            w_in = sgu_w_in[j].astype(BF16)
            w_s = sgu_w_s[j].astype(BF16)
            gw = SGU_INNER // SGU_GROUPS
            bs_exp = jnp.repeat(sgu_b_s[j].T, gw, axis=1)
            w_out = sgu_w_out[j].astype(BF16)
            lnw, lnb = row(sgu_ln_w[j]), row(sgu_ln_b[j])
            z_l = _modmm(x_lat, mul_l, ml[0], w_in, "gelu")
            x_lat = _sgu_core(z_l, x_lat, lnw, lnb, w_s, bs_exp, w_out, ml[2])
            if need_ctx:
                z_c = _modmm(x_ctx, mul_c, mc[0], w_in, "gelu")
                x_ctx = _sgu_core(z_c, x_ctx, lnw, lnb, w_s, bs_exp, w_out, mc[2])
        else:
            w_qkv = na_w_qkv[j].astype(BF16)
            scale = NA_HEAD_DIM ** -0.5
            nw = jnp.concatenate([jnp.tile(na_q_norm[j] * scale, NA_HEADS),
                                  jnp.tile(na_k_norm[j], NA_HEADS),
                                  jnp.ones((d,), F32)]).reshape(1, -1)
            w_out = na_w_out[j].astype(BF16)
            bias = _na_bias_table(na_rpb[j], x_lat.shape[0] // GRID_W)
            qkv_c = _modmm(x_ctx, mul_c, mc[0], w_qkv, "qkv", nw)
            qkv_l = _modmm(x_lat, mul_l, ml[0], w_qkv, "qkv", nw)
            o_l = _na_attention(qkv_l, qkv_c, bias)
            x_lat = _mm_res(o_l, w_out, x_lat, ml[2])
            if need_ctx:
                o_c = _ctx_attention(qkv_c)
                x_ctx = _mm_res(o_c, w_out, x_ctx, mc[2])

        nch = FFN_HIDDEN // FFN_CHUNK
        wup_c = ffn_w_up[i].astype(BF16)
        cw_c = _ffn_chunked(ffn_conv_w[i])
        cb_c = _ffn_chunked(ffn_conv_b[i].reshape(1, -1))
        wdn_c = ffn_w_down[i].astype(BF16).reshape(nch, FFN_CHUNK, d)
        x_lat = _ffn(x_lat, nw1 * (1.0 + ml[4]), ml[3], ml[5], wup_c, cw_c, cb_c, wdn_c)
        if need_ctx:
            x_ctx = _ffn(x_ctx, nw1 * (1.0 + mc[4]), mc[3], mc[5], wup_c, cw_c, cb_c, wdn_c)
    return x_lat[None]
```

```python
import functools
import math

import numpy as np
import jax
import jax.numpy as jnp
from jax import lax
from jax.experimental import pallas as pl
from jax.experimental.pallas import tpu as pltpu

F32 = jnp.float32
BF16 = jnp.bfloat16
HIGHEST = lax.Precision.HIGHEST

D_MODEL = 1024
DEPTH = 4
N_MIXERS = 3
EPS = 1e-6
GRID_W = 64
SSM_INNER = 2 * D_MODEL
SSM_HEAD_DIM = 64
SSM_HEADS = SSM_INNER // SSM_HEAD_DIM
SSM_GROUPS = 8
SSM_HPG = SSM_HEADS // SSM_GROUPS
SSM_STATE = 128
SSM_CONV = 7
SSD_CHUNK = 128
SSM_GN = SSM_GROUPS * SSM_STATE
SSM_ZXBC = 2 * SSM_INNER + 2 * SSM_GN
SGU_INNER = 2 * D_MODEL
SGU_GROUPS = 8
TOKEN_CHUNK = 128
NA_HEAD_DIM = 64
NA_HEADS = D_MODEL // NA_HEAD_DIM
NA_ROW_WIN = 8
NA_COL_WIN = 16
FFN_HIDDEN = 2816
FFN_CONV = 3

LANES = 128
BF16_SUBLANES = 16
VMEM_LIMIT = 56 * 1024 * 1024

HALO = BF16_SUBLANES
FFN_CHUNK = 256
SSD_CHUNKS_PER_STEP = 4
ROW_PITCH = 2
NA_RB = 4
NA_WR = NA_RB + NA_ROW_WIN


def _cparams(sem, flags=None):
    return pltpu.CompilerParams(dimension_semantics=sem, vmem_limit_bytes=VMEM_LIMIT, flags=flags)


def _row_tile(n, pref):
    t = min(n, pref)
    assert n % t == 0
    return t


def _sigmoid(v):
    return 1.0 / (1.0 + jnp.exp(-v))


def _silu(v):
    return v * _sigmoid(v)


def _modulate(x, mul, shift):
    ms = jnp.mean(x * x, axis=-1, keepdims=True)
    return x * lax.rsqrt(ms + EPS) * mul + shift


def _lane_lt(shape, n):
    return lax.broadcasted_iota(jnp.int32, shape, len(shape) - 1) < n


def _mod_kernel(c_ref, w_ref, b_ref, o_ref):
    s = _silu(c_ref[...])
    o_ref[0] = jnp.dot(s, w_ref[0], preferred_element_type=F32, precision=HIGHEST) + b_ref[0]


def _mod_vectors(cond, w_mod, b_mod):
    depth, d, n = w_mod.shape
    tn = 1536
    return pl.pallas_call(
        _mod_kernel,
        grid=(depth, n // tn),
        in_specs=[
            pl.BlockSpec((8, d), lambda i, j: (0, 0)),
            pl.BlockSpec((1, d, tn), lambda i, j: (i, 0, j)),
            pl.BlockSpec((1, 1, tn), lambda i, j: (i, 0, j)),
        ],
        out_specs=pl.BlockSpec((1, 8, tn), lambda i, j: (i, 0, j)),
        out_shape=jax.ShapeDtypeStruct((depth, 8, n), F32),
        compiler_params=_cparams(("arbitrary", "arbitrary")),
        name="mod_vectors",
    )(cond, w_mod, b_mod.reshape(depth, 1, n))


def _gelu_tanh(v):
    c = math.sqrt(2.0 / math.pi)
    return v * (0.5 * (1.0 + jnp.tanh(c * (v + 0.044715 * (v * v * v)))))


def _head_rmsnorm(blk, nw):
    lo = _lane_lt(blk.shape, NA_HEAD_DIM)
    sq = blk * blk
    s_lo = jnp.sum(jnp.where(lo, sq, 0.0), axis=-1, keepdims=True)
    s_hi = jnp.sum(jnp.where(lo, 0.0, sq), axis=-1, keepdims=True)
    r_lo = lax.rsqrt(s_lo * (1.0 / NA_HEAD_DIM) + EPS)
    r_hi = lax.rsqrt(s_hi * (1.0 / NA_HEAD_DIM) + EPS)
    return blk * jnp.where(lo, r_lo, r_hi) * nw


def _modmm_kernel(x_ref, mul_ref, sh_ref, w_ref, *rest, mode, tn):
    if mode == "qkv":
        nw_ref, o_ref, h_scr = rest
    else:
        o_ref, h_scr = rest
    h_scr[...] = _modulate(x_ref[...], mul_ref[...], sh_ref[...]).astype(BF16)
    n = w_ref.shape[1]
    for j in range(n // tn):
        col = slice(j * tn, (j + 1) * tn)
        acc = jnp.dot(h_scr[...], w_ref[:, col], preferred_element_type=F32)
        if mode == "gelu":
            o_ref[:, col] = _gelu_tanh(acc).astype(o_ref.dtype)
        elif j < 2 * D_MODEL // tn:
            for b in range(tn // LANES):
                sl = slice(j * tn + b * LANES, j * tn + (b + 1) * LANES)
                o_ref[:, sl] = _head_rmsnorm(acc[:, b * LANES:(b + 1) * LANES], nw_ref[:, sl]).astype(o_ref.dtype)
        else:
            o_ref[:, col] = acc.astype(o_ref.dtype)


def _modmm(x, mul, shift, w, mode, nw=None):
    rows, d = x.shape
    n = w.shape[1]
    tm = _row_tile(rows, 512)
    tn = 256
    in_specs = [
        pl.BlockSpec((tm, d), lambda i: (i, 0)),
        pl.BlockSpec((1, d), lambda i: (0, 0)),
        pl.BlockSpec((1, d), lambda i: (0, 0)),
        pl.BlockSpec((d, n), lambda i: (0, 0), pipeline_mode=pl.Buffered(1)),
    ]
    args = [x, mul, shift, w]
    if mode == "qkv":
        in_specs.append(pl.BlockSpec((1, n), lambda i: (0, 0)))
        args.append(nw)
    return pl.pallas_call(
        functools.partial(_modmm_kernel, mode=mode, tn=tn),
        grid=(rows // tm,),
        in_specs=in_specs,
        out_specs=pl.BlockSpec((tm, n), lambda i: (i, 0)),
        out_shape=jax.ShapeDtypeStruct((rows, n), BF16),
        scratch_shapes=[pltpu.VMEM((tm, d), BF16)],
        compiler_params=_cparams(("arbitrary",)),
        name="modmm_" + mode,
    )(*args)


def _mmres_kernel(a_ref, w_ref, x_ref, g_ref, o_ref):
    y = jnp.dot(a_ref[...], w_ref[...], preferred_element_type=F32)
    o_ref[...] = x_ref[...] + g_ref[...] * y


def _mm_res(a, w, x, gate):
    rows, k = a.shape
    d = w.shape[1]
    tm = _row_tile(rows, 1024)
    return pl.pallas_call(
        _mmres_kernel,
        grid=(rows // tm,),
        in_specs=[
            pl.BlockSpec((tm, k), lambda i: (i, 0)),
            pl.BlockSpec((k, d), lambda i: (0, 0)),
            pl.BlockSpec((tm, d), lambda i: (i, 0)),
            pl.BlockSpec((1, d), lambda i: (0, 0)),
        ],
        out_specs=pl.BlockSpec((tm, d), lambda i: (i, 0)),
        out_shape=jax.ShapeDtypeStruct((rows, d), F32),
        compiler_params=_cparams(("arbitrary",)),
        name="mm_res",
    )(a, w, x, gate)


def _fill_h_with_halo(h_scr, xp_ref, x_ref, xn_ref, mul, sh, tm, nblk):
    i = pl.program_id(0)
    h_scr[HALO:HALO + tm, :] = _modulate(x_ref[...], mul, sh).astype(BF16)
    hp = jnp.where(i > 0, _modulate(xp_ref[...], mul, sh), 0.0)
    hn = jnp.where(i < nblk - 1, _modulate(xn_ref[...], mul, sh), 0.0)
    h_scr[0:HALO, :] = hp.astype(BF16)
    h_scr[HALO + tm:HALO + tm + HALO, :] = hn.astype(BF16)


def _halo_specs(tm, d, rows):
    per = tm // HALO
    last = rows // HALO - 1
    return [
        pl.BlockSpec((HALO, d), lambda i, *_: (jnp.maximum(i * per - 1, 0), 0)),
        pl.BlockSpec((tm, d), lambda i, *_: (i, 0)),
        pl.BlockSpec((HALO, d), lambda i, *_: (jnp.minimum((i + 1) * per, last), 0)),
    ]


def _dwconv_from_scratch(acc_scr, cw_ref, cb_ref, taps, tm):
    base = HALO - taps // 2
    y = cb_ref[...] + cw_ref[0:1, :] * acc_scr[pl.ds(base, tm), :]
    for k in range(1, taps):
        y = y + cw_ref[k:k + 1, :] * acc_scr[pl.ds(base + k, tm), :]
    return y


def _dwconv_pitched(a, acc_scr, slab0, cw_ref, cb_ref, col0, taps, tm, finish):
    n, c = a.shape
    for l in range(c // LANES):
        acc_scr[slab0 + l, pl.ds(0, n, stride=ROW_PITCH), :] = a[:, l * LANES:(l + 1) * LANES]
    for l in range(c // LANES):
        lanes = slice(col0 + l * LANES, col0 + (l + 1) * LANES)
        y = cb_ref[:, lanes]
        for k in range(taps):
            off = HALO - taps // 2 + k
            y = y + cw_ref[k:k + 1, lanes] * acc_scr[slab0 + l, pl.ds(ROW_PITCH * off, tm, stride=ROW_PITCH), :]
        finish(lanes, y)


def _ssm_in_kernel(xp_ref, x_ref, xn_ref, mul_ref, sh_ref, w_ref, wdt_ref, cw_ref, cb_ref,
                   o_ref, dt_ref, h_scr, acc_scr, *, tm, nblk, tn):
    _fill_h_with_halo(h_scr, xp_ref, x_ref, xn_ref, mul_ref[...], sh_ref[...], tm, nblk)
    h_mid = h_scr[HALO:HALO + tm, :]
    dt_ref[...] = jnp.dot(h_mid, wdt_ref[...], preferred_element_type=F32)
    def finish(lanes, y):
        o_ref[:, lanes] = _silu(y).astype(o_ref.dtype)

    def z_tile(j):
        col = slice(j * tn, (j + 1) * tn)
        o_ref[:, col] = _silu(jnp.dot(h_mid, w_ref[:, col], preferred_element_type=F32)).astype(o_ref.dtype)

    j0 = SSM_INNER // tn
    n_conv = SSM_ZXBC // tn - j0
    assert n_conv % j0 == 0
    for i in range(n_conv):
        j = j0 + i
        col = slice(j * tn, (j + 1) * tn)
        a = jnp.dot(h_scr[...], w_ref[:, col], preferred_element_type=F32)
        if (i * j0) % n_conv < j0:
            z_tile(i * j0 // n_conv)
        _dwconv_pitched(a, acc_scr, i * (tn // LANES), cw_ref, cb_ref, j * tn, SSM_CONV, tm, finish)


def _ssm_in(x, mul, shift, w_zxbc, w_dt, conv_w, conv_b):
    rows, d = x.shape
    n = w_zxbc.shape[1]
    tm = _row_tile(rows, 512)
    tn = 256
    nblk = rows // tm
    ndt = w_dt.shape[1]
    in_specs = _halo_specs(tm, d, rows) + [
        pl.BlockSpec((1, d), lambda i: (0, 0)),
        pl.BlockSpec((1, d), lambda i: (0, 0)),
        pl.BlockSpec((d, n), lambda i: (0, 0), pipeline_mode=pl.Buffered(1)),
        pl.BlockSpec((d, ndt), lambda i: (0, 0)),
        pl.BlockSpec((SSM_CONV, n), lambda i: (0, 0)),
        pl.BlockSpec((1, n), lambda i: (0, 0)),
    ]
    return pl.pallas_call(
        functools.partial(_ssm_in_kernel, tm=tm, nblk=nblk, tn=tn),
        grid=(nblk,),
        in_specs=in_specs,
        out_specs=[
            pl.BlockSpec((tm, n), lambda i: (i, 0)),
            pl.BlockSpec((tm, ndt), lambda i: (i, 0)),
        ],
        out_shape=[
            jax.ShapeDtypeStruct((rows, n), BF16),
            jax.ShapeDtypeStruct((rows, ndt), F32),
        ],
        scratch_shapes=[
            pltpu.VMEM((tm + 2 * HALO, d), BF16),
            pltpu.VMEM(((n - SSM_INNER) // LANES, ROW_PITCH * (tm + 2 * HALO), LANES), F32),
        ],
        compiler_params=_cparams(("arbitrary",)),
        name="ssm_in",
    )(x, x, x, mul, shift, w_zxbc, w_dt, conv_w, conv_b)


def _softplus(v):
    return jnp.maximum(v, 0.0) + jnp.log1p(jnp.exp(-jnp.abs(v)))


LOG2E = 1.4426950408889634


def _scan_mask(q, rev):
    row = lax.broadcasted_iota(jnp.int32, (q, q), 0)
    col = lax.broadcasted_iota(jnp.int32, (q, q), 1)
    return (col >= row) if rev else (row >= col)


def _hi_lo(v):
    hi = v.astype(BF16)
    return hi, (v - hi.astype(F32)).astype(BF16)


def _ssd_prep_kernel(dt_ref, dtb_ref, an_ref, cum2_ref, rowt_ref, ef_ref, etot_ref, *, nck):
    q = SSD_CHUNK
    for d in range(2):
        rev = d == 1
        tri = _scan_mask(q, rev).astype(F32)
        for ck in range(nck):
            rs = slice(ck * q, (ck + 1) * q)
            dt = _softplus(dt_ref[rs, d * LANES:(d + 1) * LANES] + dtb_ref[d:d + 1, :])
            d_a = dt * an_ref[d:d + 1, :]
            cum = jnp.dot(tri, d_a, preferred_element_type=F32, precision=HIGHEST)
            tot = cum[0:1, :] if rev else cum[q - 1:q, :]
            cum2_ref[d, rs, :] = cum * LOG2E
            rowt_ref[d, ck] = ((cum - jnp.log(dt)) * LOG2E).T
            parts = _hi_lo(jnp.exp(cum)) + _hi_lo(dt * jnp.exp(tot - cum))
            for k, part in enumerate(parts):
                ef_ref[d, rs, k * LANES:(k + 1) * LANES] = part
            etot_ref[d, ck] = jnp.broadcast_to(jnp.exp(tot), (8, LANES))


def _ssd_prep(dt_raw, dt_bias, a_neg):
    rows = dt_raw.shape[0]
    q = SSD_CHUNK
    tm = _row_tile(rows, 1024)
    nck = tm // q
    nc = rows // q
    col_spec = pl.BlockSpec((2, tm, LANES), lambda i: (0, i, 0))
    return pl.pallas_call(
        functools.partial(_ssd_prep_kernel, nck=nck),
        grid=(rows // tm,),
        in_specs=[
            pl.BlockSpec((tm, 2 * LANES), lambda i: (i, 0)),
            pl.BlockSpec((2, LANES), lambda i: (0, 0)),
            pl.BlockSpec((2, LANES), lambda i: (0, 0)),
        ],
        out_specs=[
            col_spec,
            pl.BlockSpec((2, nck, LANES, q), lambda i: (0, i, 0, 0)),
            pl.BlockSpec((2, tm, 4 * LANES), lambda i: (0, i, 0)),
            pl.BlockSpec((2, nck, 8, LANES), lambda i: (0, i, 0, 0)),
        ],
        out_shape=[
            jax.ShapeDtypeStruct((2, rows, LANES), F32),
            jax.ShapeDtypeStruct((2, nc, LANES, q), F32),
            jax.ShapeDtypeStruct((2, rows, 4 * LANES), BF16),
            jax.ShapeDtypeStruct((2, nc, 8, LANES), F32),
        ],
        compiler_params=_cparams(("arbitrary",)),
        name="ssd_prep",
    )(dt_raw, dt_bias, a_neg)


def _ssd_kernel(*refs, rev, epilogue, cps):
    if epilogue:
        (x_ref, b_ref, c_ref, cum2_ref, rowt_ref, ef_ref, etot_ref, sel_ref, h0_ref,
         z_ref, yf_ref, dsk_ref, nw_ref, y_ref, hf_ref, h_scr) = refs
    else:
        (x_ref, b_ref, c_ref, cum2_ref, rowt_ref, ef_ref, etot_ref, sel_ref, h0_ref,
         y_ref, hf_ref, h_scr) = refs
    q = SSD_CHUNK
    hd = SSM_HEAD_DIM
    step = pl.program_id(0)

    @pl.when(step == 0)
    def _():
        h_scr[...] = h0_ref[...]

    tmask = _scan_mask(q, rev)
    lo = _lane_lt((q, LANES), hd)
    lo1 = _lane_lt((1, LANES), hd)
    gw = SSM_HPG * hd
    for sub in (reversed(range(cps)) if rev else range(cps)):
        rows = slice(sub * q, (sub + 1) * q)
        _ssd_chunk(refs, rows, sub, tmask, lo, lo1, gw, epilogue)

    @pl.when(step == pl.num_programs(0) - 1)
    def _():
        hf_ref[...] = h_scr[...]


def _ssd_chunk(refs, rows, sub, tmask, lo, lo1, gw, epilogue):
    if epilogue:
        (x_ref, b_ref, c_ref, cum2_ref, rowt_ref, ef_ref, etot_ref, sel_ref, h0_ref,
         z_ref, yf_ref, dsk_ref, nw_ref, y_ref, hf_ref, h_scr) = refs
    else:
        (x_ref, b_ref, c_ref, cum2_ref, rowt_ref, ef_ref, etot_ref, sel_ref, h0_ref,
         y_ref, hf_ref, h_scr) = refs
    hd = SSM_HEAD_DIM
    cum2 = cum2_ref[0, rows, :]
    rowt = rowt_ref[0, sub]
    etot = etot_ref[0, sub][0:1, :]

    for g in range(SSM_GROUPS):
        bg = b_ref[rows, g * SSM_STATE:(g + 1) * SSM_STATE]
        cg = c_ref[rows, g * SSM_STATE:(g + 1) * SSM_STATE]
        cb = lax.dot_general(cg, bg, (((1,), (1,)), ((), ())), preferred_element_type=F32)
        h_t = h_scr[g]
        esc_g = jnp.dot(ef_ref[0, rows, 0:2 * LANES], sel_ref[g], preferred_element_type=F32)
        fsc_g = jnp.dot(ef_ref[0, rows, 2 * LANES:4 * LANES], sel_ref[g], preferred_element_type=F32)
        yoff = jnp.dot(cg, h_t.astype(BF16), preferred_element_type=F32) * esc_g
        wg = (x_ref[rows, g * gw:(g + 1) * gw].astype(F32) * fsc_g).astype(BF16)
        dec_rows = []
        gated = []
        for pair in range(SSM_HPG // 2):
            ms = []
            for rr in range(2):
                h = g * SSM_HPG + 2 * pair + rr
                seg = cum2[:, h:h + 1] - rowt[h:h + 1, :]
                ms.append((cb * jnp.exp2(jnp.where(tmask, seg, -jnp.inf))).astype(BF16))
            h0i = g * SSM_HPG + 2 * pair
            c0 = g * SSM_HPG * hd + pair * LANES
            xp = x_ref[rows, c0:c0 + LANES]
            zero = jnp.zeros_like(xp)
            xcat = jnp.concatenate([jnp.where(lo, xp, zero), jnp.where(lo, zero, xp)], axis=0)
            ydiag = jnp.dot(jnp.concatenate(ms, axis=1), xcat, preferred_element_type=F32)
            y_pair = ydiag + yoff[:, pair * LANES:(pair + 1) * LANES]
            dec_rows.append(jnp.where(lo1, etot[:, h0i:h0i + 1], etot[:, h0i + 1:h0i + 2]))
            if epilogue:
                yt = y_pair + yf_ref[rows, c0:c0 + LANES].astype(F32) + dsk_ref[:, c0:c0 + LANES] * xp.astype(F32)
                gated.append(yt * z_ref[rows, c0:c0 + LANES].astype(F32))
            else:
                y_ref[rows, c0:c0 + LANES] = y_pair.astype(y_ref.dtype)
        if epilogue:
            ssq = sum(jnp.sum(v * v, axis=-1, keepdims=True) for v in gated)
            rinv = lax.rsqrt(ssq * (1.0 / (SSM_HPG * hd)) + EPS)
            for pair, v in enumerate(gated):
                c0 = g * SSM_HPG * hd + pair * LANES
                y_ref[rows, c0:c0 + LANES] = (v * rinv * nw_ref[:, c0:c0 + LANES]).astype(y_ref.dtype)
        st = lax.dot_general(bg, wg, (((0,), (0,)), ((), ())), preferred_element_type=F32)
        h_scr[g] = h_t * jnp.concatenate(dec_rows, axis=1) + st


def _head_select():
    head = np.arange(LANES)[None, :, None]
    col = np.arange(SSM_HPG * SSM_HEAD_DIM)[None, None, :]
    g = np.arange(SSM_GROUPS)[:, None, None]
    one = (head == g * SSM_HPG + col // SSM_HEAD_DIM).astype(np.float32)
    return jnp.asarray(np.concatenate([one, one], axis=1), dtype=BF16)


def _ssd_scan(zxbc, prep, h0, rev, y_other=None, d_skip=None, norm_w=None):
    rows = zxbc.shape[0]
    q = SSD_CHUNK
    nc = rows // q
    cps = SSD_CHUNKS_PER_STEP if nc % SSD_CHUNKS_PER_STEP == 0 else 1
    nsteps = nc // cps
    tq = cps * q
    epilogue = y_other is not None
    d = 1 if rev else 0

    def cidx(s):
        return (nsteps - 1 - s) if rev else s

    col_spec = pl.BlockSpec((1, tq, LANES), lambda s: (d, cidx(s), 0))
    in_specs = [
        pl.BlockSpec((tq, SSM_INNER), lambda s: (cidx(s), 1)),
        pl.BlockSpec((tq, SSM_GN), lambda s: (cidx(s), 2 * SSM_INNER // SSM_GN)),
        pl.BlockSpec((tq, SSM_GN), lambda s: (cidx(s), 2 * SSM_INNER // SSM_GN + 1)),
        col_spec,
        pl.BlockSpec((1, cps, LANES, q), lambda s: (d, cidx(s), 0, 0)),
        pl.BlockSpec((1, tq, 4 * LANES), lambda s: (d, cidx(s), 0)),
        pl.BlockSpec((1, cps, 8, LANES), lambda s: (d, cidx(s), 0, 0)),
        pl.BlockSpec((SSM_GROUPS, 2 * LANES, SSM_HPG * SSM_HEAD_DIM), lambda s: (0, 0, 0)),
        pl.BlockSpec((SSM_GROUPS, SSM_STATE, SSM_HPG * SSM_HEAD_DIM), lambda s: (0, 0, 0)),
    ]
    cum2, rowt, ef, etot = prep
    args = [zxbc, zxbc, zxbc, cum2, rowt, ef, etot, _head_select(), h0]
    if epilogue:
        in_specs += [
            pl.BlockSpec((tq, SSM_INNER), lambda s: (cidx(s), 0)),
            pl.BlockSpec((tq, SSM_INNER), lambda s: (cidx(s), 0)),
            pl.BlockSpec((1, SSM_INNER), lambda s: (0, 0)),
            pl.BlockSpec((1, SSM_INNER), lambda s: (0, 0)),
        ]
        args += [zxbc, y_other, d_skip, norm_w]
    state_shape = (SSM_GROUPS, SSM_STATE, SSM_HPG * SSM_HEAD_DIM)
    return pl.pallas_call(
        functools.partial(_ssd_kernel, rev=rev, epilogue=epilogue, cps=cps),
        grid=(nsteps,),
        in_specs=in_specs,
        out_specs=[
            pl.BlockSpec((tq, SSM_INNER), lambda s: (cidx(s), 0)),
            pl.BlockSpec(state_shape, lambda s: (0, 0, 0)),
        ],
        out_shape=[
            jax.ShapeDtypeStruct((rows, SSM_INNER), BF16),
            jax.ShapeDtypeStruct(state_shape, F32),
        ],
        scratch_shapes=[pltpu.VMEM(state_shape, F32)],
        compiler_params=_cparams(("arbitrary",)),
        name="ssd_bwd" if rev else "ssd_fwd",
    )(*args)


def _sgu_kernel(u_ref, v_ref, x_ref, lnw_ref, lnb_ref, ws_ref, bs_ref, wo_ref, g_ref, o_ref,
                vn_scr, uv_scr, *, tm):
    v = v_ref[...].astype(F32)
    mu = jnp.mean(v, axis=-1, keepdims=True)
    vc = v - mu
    var = jnp.mean(vc * vc, axis=-1, keepdims=True)
    vn_scr[...] = (vc * lax.rsqrt(var + EPS) * lnw_ref[...] + lnb_ref[...]).astype(BF16)
    gw = SGU_INNER // SGU_GROUPS
    for ch in range(tm // TOKEN_CHUNK):
        rs = slice(ch * TOKEN_CHUNK, (ch + 1) * TOKEN_CHUNK)
        for g in range(SGU_GROUPS):
            cs = slice(g * gw, (g + 1) * gw)
            sv = jnp.dot(ws_ref[g], vn_scr[rs, cs], preferred_element_type=F32) + bs_ref[:, cs]
            uv_scr[rs, cs] = (u_ref[rs, cs].astype(F32) * sv).astype(BF16)
    y = jnp.dot(uv_scr[...], wo_ref[...], preferred_element_type=F32)
    o_ref[...] = x_ref[...] + g_ref[...] * y


def _sgu_core(z, x, ln_w, ln_b, w_s, bs_exp, w_out, gate):
    rows, d = x.shape
    e = SGU_INNER
    tm = _row_tile(rows, 512)
    return pl.pallas_call(
        functools.partial(_sgu_kernel, tm=tm),
        grid=(rows // tm,),
        in_specs=[
            pl.BlockSpec((tm, e), lambda i: (i, 0)),
            pl.BlockSpec((tm, e), lambda i: (i, 1)),
            pl.BlockSpec((tm, d), lambda i: (i, 0)),
            pl.BlockSpec((1, e), lambda i: (0, 0)),
            pl.BlockSpec((1, e), lambda i: (0, 0)),
            pl.BlockSpec((SGU_GROUPS, TOKEN_CHUNK, TOKEN_CHUNK), lambda i: (0, 0, 0)),
            pl.BlockSpec((TOKEN_CHUNK, e), lambda i: (0, 0)),
            pl.BlockSpec((e, d), lambda i: (0, 0)),
            pl.BlockSpec((1, d), lambda i: (0, 0)),
        ],
        out_specs=pl.BlockSpec((tm, d), lambda i: (i, 0)),
        out_shape=jax.ShapeDtypeStruct((rows, d), F32),
        scratch_shapes=[pltpu.VMEM((tm, e), BF16), pltpu.VMEM((tm, e), BF16)],
        compiler_params=_cparams(("arbitrary",)),
        name="sgu_core",
    )(z, z, x, ln_w, ln_b, w_s, bs_exp, w_out, gate)


def _softmax_pv(s_list, v_list):
    m = s_list[0].max(axis=-1, keepdims=True)
    for s in s_list[1:]:
        m = jnp.maximum(m, s.max(axis=-1, keepdims=True))
    ps = [jnp.exp(s - m) for s in s_list]
    den = sum(p.sum(axis=-1, keepdims=True) for p in ps)
    o = sum(jnp.dot(p.astype(BF16), v, preferred_element_type=F32) for p, v in zip(ps, v_list))
    return o * (1.0 / den)


_NT = (((1,), (1,)), ((), ()))


def _na_kernel(q_ref, k_ref, v_ref, kc_ref, vc_ref, bias_ref, o_ref, *, rows, nblk, nsub):
    kc = kc_ref[...]
    vc = vc_ref[...]
    tq = NA_RB * GRID_W
    lo = _lane_lt((tq, LANES), NA_HEAD_DIM)
    for sb in range(nsub):
        rb = pl.program_id(1) * nsub + sb
        wstart = jnp.clip(rb * NA_RB - NA_ROW_WIN // 2, 0, rows - NA_WR)
        variant = jnp.where(rb == 0, 0, jnp.where(rb == nblk - 1, 2, 1))
        koff = pl.multiple_of(wstart * GRID_W, GRID_W)
        kw = k_ref[pl.ds(koff, NA_WR * GRID_W), :]
        vw = v_ref[pl.ds(koff, NA_WR * GRID_W), :]
        qv = q_ref[sb * tq:(sb + 1) * tq, :]
        zero = jnp.zeros_like(qv)
        outs = []
        for hh in range(2):
            qm = jnp.where(lo, qv, zero) if hh == 0 else jnp.where(lo, zero, qv)
            s_win = lax.dot_general(qm, kw, _NT, preferred_element_type=F32) + bias_ref[hh, variant]
            s_ctx = lax.dot_general(qm, kc, _NT, preferred_element_type=F32)
            outs.append(_softmax_pv([s_win, s_ctx], [vw, vc]))
        o_ref[sb * tq:(sb + 1) * tq, :] = jnp.where(lo, outs[0], outs[1]).astype(o_ref.dtype)


def _na_attention(qkv_lat, qkv_ctx, bias):
    n_lat = qkv_lat.shape[0]
    n_ctx = qkv_ctx.shape[0]
    rows = n_lat // GRID_W
    nblk = rows // NA_RB
    assert rows >= NA_WR and rows % NA_RB == 0
    hp_n = NA_HEADS // 2
    nsub = 4 if nblk % 4 == 0 else 1
    tq = nsub * NA_RB * GRID_W
    return pl.pallas_call(
        functools.partial(_na_kernel, rows=rows, nblk=nblk, nsub=nsub),
        grid=(hp_n, nblk // nsub),
        in_specs=[
            pl.BlockSpec((tq, LANES), lambda hp, rb: (rb, hp)),
            pl.BlockSpec((n_lat, LANES), lambda hp, rb: (0, hp_n + hp)),
            pl.BlockSpec((n_lat, LANES), lambda hp, rb: (0, 2 * hp_n + hp)),
            pl.BlockSpec((n_ctx, LANES), lambda hp, rb: (0, hp_n + hp)),
            pl.BlockSpec((n_ctx, LANES), lambda hp, rb: (0, 2 * hp_n + hp)),
            pl.BlockSpec((2, 3, NA_RB * GRID_W, NA_WR * GRID_W), lambda hp, rb: (hp, 0, 0, 0)),
        ],
        out_specs=pl.BlockSpec((tq, LANES), lambda hp, rb: (rb, hp)),
        out_shape=jax.ShapeDtypeStruct((n_lat, D_MODEL), BF16),
        compiler_params=_cparams(("arbitrary", "arbitrary")),
        name="na_attention",
    )(qkv_lat, qkv_lat, qkv_lat, qkv_ctx, qkv_ctx, bias)


def _ctx_attn_kernel(q_ref, k_ref, v_ref, o_ref):
    qv = q_ref[...]
    lo = _lane_lt(qv.shape, NA_HEAD_DIM)
    zero = jnp.zeros_like(qv)
    outs = []
    for hh in range(2):
        qm = jnp.where(lo, qv, zero) if hh == 0 else jnp.where(lo, zero, qv)
        s = lax.dot_general(qm, k_ref[...], _NT, preferred_element_type=F32)
        outs.append(_softmax_pv([s], [v_ref[...]]))
    o_ref[...] = jnp.where(lo, outs[0], outs[1]).astype(o_ref.dtype)


def _ctx_attention(qkv_ctx):
    n_ctx = qkv_ctx.shape[0]
    hp_n = NA_HEADS // 2
    return pl.pallas_call(
        _ctx_attn_kernel,
        grid=(hp_n,),
        in_specs=[
            pl.BlockSpec((n_ctx, LANES), lambda hp: (0, hp)),
            pl.BlockSpec((n_ctx, LANES), lambda hp: (0, hp_n + hp)),
            pl.BlockSpec((n_ctx, LANES), lambda hp: (0, 2 * hp_n + hp)),
        ],
        out_specs=pl.BlockSpec((n_ctx, LANES), lambda hp: (0, hp)),
        out_shape=jax.ShapeDtypeStruct((n_ctx, D_MODEL), BF16),
        compiler_params=_cparams(("arbitrary",)),
        name="ctx_attention",
    )(qkv_ctx, qkv_ctx, qkv_ctx)


def _na_bias_table(rpb, rows):
    col = np.arange(GRID_W)
    col_start = np.clip(col - NA_COL_WIN // 2, 0, GRID_W - NA_COL_WIN)
    in_win = (col[None, :] >= col_start[:, None]) & (col[None, :] < col_start[:, None] + NA_COL_WIN)
    w = GRID_W
    edge = w - NA_COL_WIN
    rp = jnp.pad(rpb.astype(F32), ((0, 0), (0, 0), (edge, edge)))
    col_bias = jnp.stack([rp[:, :, w - 1 - qc:2 * w - 1 - qc] for qc in range(w)], axis=2)
    col_bias = jnp.where(in_win, col_bias, -jnp.inf)
    cbt = jnp.transpose(col_bias, (0, 2, 1, 3)).reshape(NA_HEADS, w, (2 * NA_ROW_WIN - 1) * w)
    wr = NA_ROW_WIN
    assert rows >= NA_WR
    variants = []
    for r0 in (0, NA_RB, rows - NA_RB):
        wstart = int(np.clip(r0 - NA_ROW_WIN // 2, 0, rows - NA_WR))
        blocks = []
        for qr in range(NA_RB):
            r = r0 + qr
            rs = int(np.clip(r - wr // 2, 0, rows - wr))
            first = rs - wstart
            a0 = rs - r + NA_ROW_WIN - 1
            blk = cbt[:, :, a0 * w:(a0 + wr) * w]
            blocks.append(jnp.pad(blk, ((0, 0), (0, 0), (first * w, (NA_WR - first - wr) * w)),
                                  constant_values=-jnp.inf))
        variants.append(jnp.concatenate(blocks, axis=1))
    return jnp.stack(variants, axis=1)


def _ffn_kernel(xp_ref, x_ref, xn_ref, mul_ref, sh_ref, gate_ref, wup_ref, cw_ref, cb_ref, wdn_ref,
                o_ref, h_scr, acc_scr, act_scr, *, tm, nblk):
    _fill_h_with_halo(h_scr, xp_ref, x_ref, xn_ref, mul_ref[...], sh_ref[...], tm, nblk)
    hc = FFN_CHUNK
    nch = FFN_HIDDEN // hc
    def up(c):
        for half in range(2):
            col = slice(half * FFN_HIDDEN + c * hc, half * FFN_HIDDEN + (c + 1) * hc)
            acc_scr[c, :, half * hc:(half + 1) * hc] = jnp.dot(
                h_scr[...], wup_ref[:, col], preferred_element_type=F32)

    lead = 2
    for c in range(lead):
        up(c)
    for c in range(nch):
        if c + lead < nch:
            up(c + lead)
        a = _dwconv_from_scratch(acc_scr.at[c], cw_ref.at[c], cb_ref.at[c], FFN_CONV, tm)
        act_scr[c] = (_silu(a[:, :hc]) * a[:, hc:]).astype(BF16)
    y = jnp.dot(act_scr[0], wdn_ref[0], preferred_element_type=F32)
    for c in range(1, nch):
        y = y + jnp.dot(act_scr[c], wdn_ref[c], preferred_element_type=F32)
    o_ref[...] = x_ref[...] + gate_ref[...] * y


def _ffn(x, mul, shift, gate, wup, cw_c, cb_c, wdn_c):
    rows, d = x.shape
    nch, _, hc2 = cw_c.shape
    tm = _row_tile(rows, 512)
    nblk = rows // tm
    in_specs = _halo_specs(tm, d, rows) + [
        pl.BlockSpec((1, d), lambda i: (0, 0)),
        pl.BlockSpec((1, d), lambda i: (0, 0)),
        pl.BlockSpec((1, d), lambda i: (0, 0)),
        pl.BlockSpec((d, 2 * FFN_HIDDEN), lambda i: (0, 0), pipeline_mode=pl.Buffered(1)),
        pl.BlockSpec((nch, FFN_CONV, hc2), lambda i: (0, 0, 0)),
        pl.BlockSpec((nch, 1, hc2), lambda i: (0, 0, 0)),
        pl.BlockSpec((nch, hc2 // 2, d), lambda i: (0, 0, 0), pipeline_mode=pl.Buffered(1)),
    ]
    return pl.pallas_call(
        functools.partial(_ffn_kernel, tm=tm, nblk=nblk),
        grid=(nblk,),
        in_specs=in_specs,
        out_specs=pl.BlockSpec((tm, d), lambda i: (i, 0)),
        out_shape=jax.ShapeDtypeStruct((rows, d), F32),
        scratch_shapes=[
            pltpu.VMEM((tm + 2 * HALO, d), BF16),
            pltpu.VMEM((nch, tm + 2 * HALO, hc2), F32),
            pltpu.VMEM((nch, tm, hc2 // 2), BF16),
        ],
        compiler_params=_cparams(("arbitrary",)),
        name="conv_ffn",
    )(x, x, x, mul, shift, gate, wup, cw_c, cb_c, wdn_c)


def _ffn_chunked(w):
    lead = w.shape[:-1]
    nch = FFN_HIDDEN // FFN_CHUNK
    w2 = w.reshape(lead + (2, nch, FFN_CHUNK))
    w2 = jnp.moveaxis(w2, -2, 0)
    return w2.reshape((nch,) + lead + (2 * FFN_CHUNK,))


def _pad_lanes(v, n):
    return jnp.pad(v, [(0, 0)] * (v.ndim - 1) + [(0, n - v.shape[-1])])


def kernel(x, c, ctx, c_ctx, norm_w, w_mod, b_mod, ssm_w_in, ssm_conv_w, ssm_conv_b, ssm_a_log, ssm_dt_bias, ssm_d_skip, ssm_norm_w, ssm_w_out, sgu_w_in, sgu_ln_w, sgu_ln_b, sgu_w_s, sgu_b_s, sgu_w_out, na_w_qkv, na_q_norm, na_k_norm, na_rpb, na_w_out, ffn_w_up, ffn_conv_w, ffn_conv_b, ffn_w_down):
    assert x.shape[0] == 1 and c.shape[0] == 1
    d = D_MODEL
    depth = w_mod.shape[0]
    x_lat = x[0]
    x_ctx = ctx[0]
    cond = jnp.zeros((8, d), F32).at[0].set(c[0]).at[1].set(c_ctx)
    mods = _mod_vectors(cond, w_mod, b_mod)

    def row(v):
        return v.reshape(1, -1)

    for i in range(depth):
        kind, j = i % N_MIXERS, i // N_MIXERS
        need_ctx = i < depth - 1
        ml = [row(mods[i, 0, k * d:(k + 1) * d]) for k in range(6)]
        mc = [row(mods[i, 1, k * d:(k + 1) * d]) for k in range(6)]
        nw0, nw1 = row(norm_w[i, 0]), row(norm_w[i, 1])
        mul_l, mul_c = nw0 * (1.0 + ml[1]), nw0 * (1.0 + mc[1])

        if kind == 0:
            w_in = ssm_w_in[j]
            w_zxbc = w_in[:, :SSM_ZXBC].astype(BF16)
            w_dt = w_in[:, SSM_ZXBC:]
            w_dt = jnp.concatenate([_pad_lanes(w_dt[:, :SSM_HEADS], LANES),
                                    _pad_lanes(w_dt[:, SSM_HEADS:], LANES)], axis=1).astype(BF16)
            cw = jnp.concatenate([jnp.zeros((SSM_CONV, SSM_INNER), F32), ssm_conv_w[j]], axis=1)
            cb = jnp.concatenate([jnp.zeros((SSM_INNER,), F32), ssm_conv_b[j]]).reshape(1, -1)
            a_neg = _pad_lanes(-jnp.exp(ssm_a_log[j]), LANES)
            dtb = _pad_lanes(ssm_dt_bias[j], LANES)
            dsk = row(jnp.repeat(ssm_d_skip[j, 0] + ssm_d_skip[j, 1], SSM_HEAD_DIM))
            gnw = row(ssm_norm_w[j])
            w_out = ssm_w_out[j].astype(BF16)
            h0 = jnp.zeros((SSM_GROUPS, SSM_STATE, SSM_HPG * SSM_HEAD_DIM), F32)

            def mixer(xs, mul, shift, hf0, hb0):
                zxbc, dt_raw = _ssm_in(xs, mul, shift, w_zxbc, w_dt, cw, cb)
                prep = _ssd_prep(dt_raw, dtb, a_neg)
                yf, hf = _ssd_scan(zxbc, prep, hf0, False)
                gn, hb = _ssd_scan(zxbc, prep, hb0, True, y_other=yf, d_skip=dsk, norm_w=gnw)
                return gn, hf, hb

            gn_c, hf, hb = mixer(x_ctx, mul_c, mc[0], h0, h0)
            gn_l, _, _ = mixer(x_lat, mul_l, ml[0], hf, hb)
            x_lat = _mm_res(gn_l, w_out, x_lat, ml[2])
            if need_ctx:
                x_ctx = _mm_res(gn_c, w_out, x_ctx, mc[2])
        elif kind == 1:
            w_in = sgu_w_in[j].astype(BF16)
            w_s = sgu_w_s[j].astype(BF16)
            gw = SGU_INNER // SGU_GROUPS
            bs_exp = jnp.repeat(sgu_b_s[j].T, gw, axis=1)
            w_out = sgu_w_out[j].astype(BF16)
            lnw, lnb = row(sgu_ln_w[j]), row(sgu_ln_b[j])
            z_l = _modmm(x_lat, mul_l, ml[0], w_in, "gelu")
            x_lat = _sgu_core(z_l, x_lat, lnw, lnb, w_s, bs_exp, w_out, ml[2])
            if need_ctx:
                z_c = _modmm(x_ctx, mul_c, mc[0], w_in, "gelu")
                x_ctx = _sgu_core(z_c, x_ctx, lnw, lnb, w_s, bs_exp, w_out, mc[2])
        else:
            w_qkv = na_w_qkv[j].astype(BF16)
            scale = NA_HEAD_DIM ** -0.5
            nw = jnp.concatenate([jnp.tile(na_q_norm[j] * scale, NA_HEADS),
                                  jnp.tile(na_k_norm[j], NA_HEADS),
                                  jnp.ones((d,), F32)]).reshape(1, -1)
            w_out = na_w_out[j].astype(BF16)
            bias = _na_bias_table(na_rpb[j], x_lat.shape[0] // GRID_W)
            qkv_c = _modmm(x_ctx, mul_c, mc[0], w_qkv, "qkv", nw)
            qkv_l = _modmm(x_lat, mul_l, ml[0], w_qkv, "qkv", nw)
            o_l = _na_attention(qkv_l, qkv_c, bias)
            x_lat = _mm_res(o_l, w_out, x_lat, ml[2])
            if need_ctx:
                o_c = _ctx_attention(qkv_c)
                x_ctx = _mm_res(o_c, w_out, x_ctx, mc[2])

        nch = FFN_HIDDEN // FFN_CHUNK
        wup_c = ffn_w_up[i].astype(BF16)
        cw_c = _ffn_chunked(ffn_conv_w[i])
        cb_c = _ffn_chunked(ffn_conv_b[i].reshape(1, -1))
        wdn_c = ffn_w_down[i].astype(BF16).reshape(nch, FFN_CHUNK, d)
        x_lat = _ffn(x_lat, nw1 * (1.0 + ml[4]), ml[3], ml[5], wup_c, cw_c, cb_c, wdn_c)
        if need_ctx:
            x_ctx = _ffn(x_ctx, nw1 * (1.0 + mc[4]), mc[3], mc[5], wup_c, cw_c, cb_c, wdn_c)
    return x_lat[None]
```

```python
import functools
import math

import numpy as np
import jax
import jax.numpy as jnp
from jax import lax
from jax.experimental import pallas as pl
from jax.experimental.pallas import tpu as pltpu

F32 = jnp.float32
BF16 = jnp.bfloat16
HIGHEST = lax.Precision.HIGHEST

D_MODEL = 1024
DEPTH = 4
N_MIXERS = 3
EPS = 1e-6
GRID_W = 64
SSM_INNER = 2 * D_MODEL
SSM_HEAD_DIM = 64
SSM_HEADS = SSM_INNER // SSM_HEAD_DIM
SSM_GROUPS = 8
SSM_HPG = SSM_HEADS // SSM_GROUPS
SSM_STATE = 128
SSM_CONV = 7
SSD_CHUNK = 128
SSM_GN = SSM_GROUPS * SSM_STATE
SSM_ZXBC = 2 * SSM_INNER + 2 * SSM_GN
SGU_INNER = 2 * D_MODEL
SGU_GROUPS = 8
TOKEN_CHUNK = 128
NA_HEAD_DIM = 64
NA_HEADS = D_MODEL // NA_HEAD_DIM
NA_ROW_WIN = 8
NA_COL_WIN = 16
FFN_HIDDEN = 2816
FFN_CONV = 3

LANES = 128
BF16_SUBLANES = 16
VMEM_LIMIT = 56 * 1024 * 1024

HALO = BF16_SUBLANES
FFN_CHUNK = 256
SSD_CHUNKS_PER_STEP = 8
ROW_PITCH = 2
NA_RB = 4
NA_WR = NA_RB + NA_ROW_WIN


def _cparams(sem, flags=None):
    return pltpu.CompilerParams(dimension_semantics=sem, vmem_limit_bytes=VMEM_LIMIT, flags=flags)


def _row_tile(n, pref):
    t = min(n, pref)
    assert n % t == 0
    return t


def _sigmoid(v):
    return 1.0 / (1.0 + jnp.exp(-v))


def _silu(v):
    return v * _sigmoid(v)


def _modulate(x, mul, shift):
    ms = jnp.mean(x * x, axis=-1, keepdims=True)
    return x * lax.rsqrt(ms + EPS) * mul + shift


def _lane_lt(shape, n):
    return lax.broadcasted_iota(jnp.int32, shape, len(shape) - 1) < n


def _mod_kernel(c_ref, w_ref, b_ref, o_ref):
    s = _silu(c_ref[...])
    o_ref[0] = jnp.dot(s, w_ref[0], preferred_element_type=F32, precision=HIGHEST) + b_ref[0]


def _mod_vectors(cond, w_mod, b_mod):
    depth, d, n = w_mod.shape
    tn = 1536
    return pl.pallas_call(
        _mod_kernel,
        grid=(depth, n // tn),
        in_specs=[
            pl.BlockSpec((8, d), lambda i, j: (0, 0)),
            pl.BlockSpec((1, d, tn), lambda i, j: (i, 0, j)),
            pl.BlockSpec((1, 1, tn), lambda i, j: (i, 0, j)),
        ],
        out_specs=pl.BlockSpec((1, 8, tn), lambda i, j: (i, 0, j)),
        out_shape=jax.ShapeDtypeStruct((depth, 8, n), F32),
        compiler_params=_cparams(("arbitrary", "arbitrary")),
        name="mod_vectors",
    )(cond, w_mod, b_mod.reshape(depth, 1, n))


def _gelu_tanh(v):
    c = math.sqrt(2.0 / math.pi)
    return v * (0.5 * (1.0 + jnp.tanh(c * (v + 0.044715 * (v * v * v)))))


def _head_rmsnorm(blk, nw):
    lo = _lane_lt(blk.shape, NA_HEAD_DIM)
    sq = blk * blk
    s_lo = jnp.sum(jnp.where(lo, sq, 0.0), axis=-1, keepdims=True)
    s_hi = jnp.sum(jnp.where(lo, 0.0, sq), axis=-1, keepdims=True)
    r_lo = lax.rsqrt(s_lo * (1.0 / NA_HEAD_DIM) + EPS)
    r_hi = lax.rsqrt(s_hi * (1.0 / NA_HEAD_DIM) + EPS)
    return blk * jnp.where(lo, r_lo, r_hi) * nw


def _modmm_kernel(x_ref, mul_ref, sh_ref, w_ref, *rest, mode, tn):
    if mode == "qkv":
        nw_ref, o_ref, h_scr = rest
    else:
        o_ref, h_scr = rest
    h_scr[...] = _modulate(x_ref[...], mul_ref[...], sh_ref[...]).astype(BF16)
    n = w_ref.shape[1]
    for j in range(n // tn):
        col = slice(j * tn, (j + 1) * tn)
        acc = jnp.dot(h_scr[...], w_ref[:, col], preferred_element_type=F32)
        if mode == "gelu":
            o_ref[:, col] = _gelu_tanh(acc).astype(o_ref.dtype)
        elif j < 2 * D_MODEL // tn:
            for b in range(tn // LANES):
                sl = slice(j * tn + b * LANES, j * tn + (b + 1) * LANES)
                o_ref[:, sl] = _head_rmsnorm(acc[:, b * LANES:(b + 1) * LANES], nw_ref[:, sl]).astype(o_ref.dtype)
        else:
            o_ref[:, col] = acc.astype(o_ref.dtype)


def _modmm(x, mul, shift, w, mode, nw=None):
    rows, d = x.shape
    n = w.shape[1]
    tm = _row_tile(rows, 1024)
    tn = 256
    in_specs = [
        pl.BlockSpec((tm, d), lambda i: (i, 0)),
        pl.BlockSpec((1, d), lambda i: (0, 0)),
        pl.BlockSpec((1, d), lambda i: (0, 0)),
        pl.BlockSpec((d, n), lambda i: (0, 0), pipeline_mode=pl.Buffered(1)),
    ]
    args = [x, mul, shift, w]
    if mode == "qkv":
        in_specs.append(pl.BlockSpec((1, n), lambda i: (0, 0)))
        args.append(nw)
    return pl.pallas_call(
        functools.partial(_modmm_kernel, mode=mode, tn=tn),
        grid=(rows // tm,),
        in_specs=in_specs,
        out_specs=pl.BlockSpec((tm, n), lambda i: (i, 0)),
        out_shape=jax.ShapeDtypeStruct((rows, n), BF16),
        scratch_shapes=[pltpu.VMEM((tm, d), BF16)],
        compiler_params=_cparams(("arbitrary",)),
        name="modmm_" + mode,
    )(*args)


def _mmres_kernel(a_ref, w_ref, x_ref, g_ref, o_ref):
    y = jnp.dot(a_ref[...], w_ref[...], preferred_element_type=F32)
    o_ref[...] = x_ref[...] + g_ref[...] * y


def _mm_res(a, w, x, gate):
    rows, k = a.shape
    d = w.shape[1]
    tm = _row_tile(rows, 1024)
    return pl.pallas_call(
        _mmres_kernel,
        grid=(rows // tm,),
        in_specs=[
            pl.BlockSpec((tm, k), lambda i: (i, 0)),
            pl.BlockSpec((k, d), lambda i: (0, 0)),
            pl.BlockSpec((tm, d), lambda i: (i, 0)),
            pl.BlockSpec((1, d), lambda i: (0, 0)),
        ],
        out_specs=pl.BlockSpec((tm, d), lambda i: (i, 0)),
        out_shape=jax.ShapeDtypeStruct((rows, d), F32),
        compiler_params=_cparams(("arbitrary",)),
        name="mm_res",
    )(a, w, x, gate)


def _fill_h_with_halo(h_scr, xp_ref, x_ref, xn_ref, mul, sh, tm, nblk):
    i = pl.program_id(0)
    h_scr[HALO:HALO + tm, :] = _modulate(x_ref[...], mul, sh).astype(BF16)
    hp = jnp.where(i > 0, _modulate(xp_ref[...], mul, sh), 0.0)
    hn = jnp.where(i < nblk - 1, _modulate(xn_ref[...], mul, sh), 0.0)
    h_scr[0:HALO, :] = hp.astype(BF16)
    h_scr[HALO + tm:HALO + tm + HALO, :] = hn.astype(BF16)


def _halo_specs(tm, d, rows):
    per = tm // HALO
    last = rows // HALO - 1
    return [
        pl.BlockSpec((HALO, d), lambda i, *_: (jnp.maximum(i * per - 1, 0), 0)),
        pl.BlockSpec((tm, d), lambda i, *_: (i, 0)),
        pl.BlockSpec((HALO, d), lambda i, *_: (jnp.minimum((i + 1) * per, last), 0)),
    ]


def _dwconv_from_scratch(acc_scr, cw_ref, cb_ref, taps, tm):
    base = HALO - taps // 2
    y = cb_ref[...] + cw_ref[0:1, :] * acc_scr[pl.ds(base, tm), :]
    for k in range(1, taps):
        y = y + cw_ref[k:k + 1, :] * acc_scr[pl.ds(base + k, tm), :]
    return y


def _dwconv_pitched(a, acc_scr, slab0, cw_ref, cb_ref, col0, taps, tm, finish):
    n, c = a.shape
    for l in range(c // LANES):
        acc_scr[slab0 + l, pl.ds(0, n, stride=ROW_PITCH), :] = a[:, l * LANES:(l + 1) * LANES]
    for l in range(c // LANES):
        lanes = slice(col0 + l * LANES, col0 + (l + 1) * LANES)
        y = cb_ref[:, lanes]
        for k in range(taps):
            off = HALO - taps // 2 + k
            y = y + cw_ref[k:k + 1, lanes] * acc_scr[slab0 + l, pl.ds(ROW_PITCH * off, tm, stride=ROW_PITCH), :]
        finish(lanes, y)


def _ssm_in_kernel(xp_ref, x_ref, xn_ref, mul_ref, sh_ref, w_ref, wdt_ref, cw_ref, cb_ref,
                   o_ref, dt_ref, h_scr, acc_scr, *, tm, nblk, tn):
    _fill_h_with_halo(h_scr, xp_ref, x_ref, xn_ref, mul_ref[...], sh_ref[...], tm, nblk)
    h_mid = h_scr[HALO:HALO + tm, :]
    dt_ref[...] = jnp.dot(h_mid, wdt_ref[...], preferred_element_type=F32)
    def finish(lanes, y):
        o_ref[:, lanes] = _silu(y).astype(o_ref.dtype)

    def z_tile(j):
        col = slice(j * tn, (j + 1) * tn)
        o_ref[:, col] = _silu(jnp.dot(h_mid, w_ref[:, col], preferred_element_type=F32)).astype(o_ref.dtype)

    j0 = SSM_INNER // tn
    n_conv = SSM_ZXBC // tn - j0
    assert n_conv % j0 == 0
    for i in range(n_conv):
        j = j0 + i
        col = slice(j * tn, (j + 1) * tn)
        a = jnp.dot(h_scr[...], w_ref[:, col], preferred_element_type=F32)
        if (i * j0) % n_conv < j0:
            z_tile(i * j0 // n_conv)
        _dwconv_pitched(a, acc_scr, i * (tn // LANES), cw_ref, cb_ref, j * tn, SSM_CONV, tm, finish)


def _ssm_in(x, mul, shift, w_zxbc, w_dt, conv_w, conv_b):
    rows, d = x.shape
    n = w_zxbc.shape[1]
    tm = _row_tile(rows, 512)
    tn = 256
    nblk = rows // tm
    ndt = w_dt.shape[1]
    in_specs = _halo_specs(tm, d, rows) + [
        pl.BlockSpec((1, d), lambda i: (0, 0)),
        pl.BlockSpec((1, d), lambda i: (0, 0)),
        pl.BlockSpec((d, n), lambda i: (0, 0), pipeline_mode=pl.Buffered(1)),
        pl.BlockSpec((d, ndt), lambda i: (0, 0)),
        pl.BlockSpec((SSM_CONV, n), lambda i: (0, 0)),
        pl.BlockSpec((1, n), lambda i: (0, 0)),
    ]
    return pl.pallas_call(
        functools.partial(_ssm_in_kernel, tm=tm, nblk=nblk, tn=tn),
        grid=(nblk,),
        in_specs=in_specs,
        out_specs=[
            pl.BlockSpec((tm, n), lambda i: (i, 0)),
            pl.BlockSpec((tm, ndt), lambda i: (i, 0)),
        ],
        out_shape=[
            jax.ShapeDtypeStruct((rows, n), BF16),
            jax.ShapeDtypeStruct((rows, ndt), F32),
        ],
        scratch_shapes=[
            pltpu.VMEM((tm + 2 * HALO, d), BF16),
            pltpu.VMEM(((n - SSM_INNER) // LANES, ROW_PITCH * (tm + 2 * HALO), LANES), F32),
        ],
        compiler_params=_cparams(("arbitrary",)),
        name="ssm_in",
    )(x, x, x, mul, shift, w_zxbc, w_dt, conv_w, conv_b)


def _softplus(v):
    return jnp.maximum(v, 0.0) + jnp.log1p(jnp.exp(-jnp.abs(v)))


LOG2E = 1.4426950408889634


def _scan_mask(q, rev):
    row = lax.broadcasted_iota(jnp.int32, (q, q), 0)
    col = lax.broadcasted_iota(jnp.int32, (q, q), 1)
    return (col >= row) if rev else (row >= col)


def _hi_lo(v):
    hi = v.astype(BF16)
    return hi, (v - hi.astype(F32)).astype(BF16)


def _ssd_prep_kernel(dt_ref, dtb_ref, an_ref, cum2_ref, rowt_ref, ef_ref, etot_ref, *, nck):
    q = SSD_CHUNK
    for d in range(2):
        rev = d == 1
        tri = _scan_mask(q, rev).astype(F32)
        for ck in range(nck):
            rs = slice(ck * q, (ck + 1) * q)
            dt = _softplus(dt_ref[rs, d * LANES:(d + 1) * LANES] + dtb_ref[d:d + 1, :])
            d_a = dt * an_ref[d:d + 1, :]
            cum = jnp.dot(tri, d_a, preferred_element_type=F32, precision=HIGHEST)
            tot = cum[0:1, :] if rev else cum[q - 1:q, :]
            cum2_ref[d, rs, :] = cum * LOG2E
            rowt_ref[d, ck] = ((cum - jnp.log(dt)) * LOG2E).T
            parts = _hi_lo(jnp.exp(cum)) + _hi_lo(dt * jnp.exp(tot - cum))
            for k, part in enumerate(parts):
                ef_ref[d, rs, k * LANES:(k + 1) * LANES] = part
            etot_ref[d, ck] = jnp.broadcast_to(jnp.exp(tot), (8, LANES))


def _ssd_prep(dt_raw, dt_bias, a_neg):
    rows = dt_raw.shape[0]
    q = SSD_CHUNK
    tm = _row_tile(rows, 1024)
    nck = tm // q
    nc = rows // q
    col_spec = pl.BlockSpec((2, tm, LANES), lambda i: (0, i, 0))
    return pl.pallas_call(
        functools.partial(_ssd_prep_kernel, nck=nck),
        grid=(rows // tm,),
        in_specs=[
            pl.BlockSpec((tm, 2 * LANES), lambda i: (i, 0)),
            pl.BlockSpec((2, LANES), lambda i: (0, 0)),
            pl.BlockSpec((2, LANES), lambda i: (0, 0)),
        ],
        out_specs=[
            col_spec,
            pl.BlockSpec((2, nck, LANES, q), lambda i: (0, i, 0, 0)),
            pl.BlockSpec((2, tm, 4 * LANES), lambda i: (0, i, 0)),
            pl.BlockSpec((2, nck, 8, LANES), lambda i: (0, i, 0, 0)),
        ],
        out_shape=[
            jax.ShapeDtypeStruct((2, rows, LANES), F32),
            jax.ShapeDtypeStruct((2, nc, LANES, q), F32),
            jax.ShapeDtypeStruct((2, rows, 4 * LANES), BF16),
            jax.ShapeDtypeStruct((2, nc, 8, LANES), F32),
        ],
        compiler_params=_cparams(("arbitrary",)),
        name="ssd_prep",
    )(dt_raw, dt_bias, a_neg)


def _ssd_kernel(*refs, rev, epilogue, cps):
    if epilogue:
        (x_ref, b_ref, c_ref, cum2_ref, rowt_ref, ef_ref, etot_ref, sel_ref, h0_ref,
         z_ref, yf_ref, dsk_ref, nw_ref, y_ref, hf_ref, h_scr) = refs
    else:
        (x_ref, b_ref, c_ref, cum2_ref, rowt_ref, ef_ref, etot_ref, sel_ref, h0_ref,
         y_ref, hf_ref, h_scr) = refs
    q = SSD_CHUNK
    hd = SSM_HEAD_DIM
    step = pl.program_id(0)

    @pl.when(step == 0)
    def _():
        h_scr[...] = h0_ref[...]

    tmask = _scan_mask(q, rev)
    lo = _lane_lt((q, LANES), hd)
    lo1 = _lane_lt((1, LANES), hd)
    gw = SSM_HPG * hd
    for sub in (reversed(range(cps)) if rev else range(cps)):
        rows = slice(sub * q, (sub + 1) * q)
        _ssd_chunk(refs, rows, sub, tmask, lo, lo1, gw, epilogue)

    @pl.when(step == pl.num_programs(0) - 1)
    def _():
        hf_ref[...] = h_scr[...]


def _ssd_chunk(refs, rows, sub, tmask, lo, lo1, gw, epilogue):
    if epilogue:
        (x_ref, b_ref, c_ref, cum2_ref, rowt_ref, ef_ref, etot_ref, sel_ref, h0_ref,
         z_ref, yf_ref, dsk_ref, nw_ref, y_ref, hf_ref, h_scr) = refs
    else:
        (x_ref, b_ref, c_ref, cum2_ref, rowt_ref, ef_ref, etot_ref, sel_ref, h0_ref,
         y_ref, hf_ref, h_scr) = refs
    hd = SSM_HEAD_DIM
    cum2 = cum2_ref[0, rows, :]
    rowt = rowt_ref[0, sub]
    etot = etot_ref[0, sub][0:1, :]

    for g in range(SSM_GROUPS):
        bg = b_ref[rows, g * SSM_STATE:(g + 1) * SSM_STATE]
        cg = c_ref[rows, g * SSM_STATE:(g + 1) * SSM_STATE]
        cb = lax.dot_general(cg, bg, (((1,), (1,)), ((), ())), preferred_element_type=F32)
        h_t = h_scr[g]
        esc_g = jnp.dot(ef_ref[0, rows, 0:2 * LANES], sel_ref[g], preferred_element_type=F32)
        fsc_g = jnp.dot(ef_ref[0, rows, 2 * LANES:4 * LANES], sel_ref[g], preferred_element_type=F32)
        yoff = jnp.dot(cg, h_t.astype(BF16), preferred_element_type=F32) * esc_g
        wg = (x_ref[rows, g * gw:(g + 1) * gw].astype(F32) * fsc_g).astype(BF16)
        dec_rows = []
        gated = []
        for pair in range(SSM_HPG // 2):
            ms = []
            for rr in range(2):
                h = g * SSM_HPG + 2 * pair + rr
                seg = cum2[:, h:h + 1] - rowt[h:h + 1, :]
                ms.append((cb * jnp.exp2(jnp.where(tmask, seg, -jnp.inf))).astype(BF16))
            h0i = g * SSM_HPG + 2 * pair
            c0 = g * SSM_HPG * hd + pair * LANES
            xp = x_ref[rows, c0:c0 + LANES]
            zero = jnp.zeros_like(xp)
            xcat = jnp.concatenate([jnp.where(lo, xp, zero), jnp.where(lo, zero, xp)], axis=0)
            ydiag = jnp.dot(jnp.concatenate(ms, axis=1), xcat, preferred_element_type=F32)
            y_pair = ydiag + yoff[:, pair * LANES:(pair + 1) * LANES]
            dec_rows.append(jnp.where(lo1, etot[:, h0i:h0i + 1], etot[:, h0i + 1:h0i + 2]))
            if epilogue:
                yt = y_pair + yf_ref[rows, c0:c0 + LANES].astype(F32) + dsk_ref[:, c0:c0 + LANES] * xp.astype(F32)
                gated.append(yt * z_ref[rows, c0:c0 + LANES].astype(F32))
            else:
                y_ref[rows, c0:c0 + LANES] = y_pair.astype(y_ref.dtype)
        if epilogue:
            ssq = sum(jnp.sum(v * v, axis=-1, keepdims=True) for v in gated)
            rinv = lax.rsqrt(ssq * (1.0 / (SSM_HPG * hd)) + EPS)
            for pair, v in enumerate(gated):
                c0 = g * SSM_HPG * hd + pair * LANES
                y_ref[rows, c0:c0 + LANES] = (v * rinv * nw_ref[:, c0:c0 + LANES]).astype(y_ref.dtype)
        st = lax.dot_general(bg, wg, (((0,), (0,)), ((), ())), preferred_element_type=F32)
        h_scr[g] = h_t * jnp.concatenate(dec_rows, axis=1) + st


def _head_select():
    head = np.arange(LANES)[None, :, None]
    col = np.arange(SSM_HPG * SSM_HEAD_DIM)[None, None, :]
    g = np.arange(SSM_GROUPS)[:, None, None]
    one = (head == g * SSM_HPG + col // SSM_HEAD_DIM).astype(np.float32)
    return jnp.asarray(np.concatenate([one, one], axis=1), dtype=BF16)


def _ssd_scan(zxbc, prep, h0, rev, y_other=None, d_skip=None, norm_w=None):
    rows = zxbc.shape[0]
    q = SSD_CHUNK
    nc = rows // q
    cps = SSD_CHUNKS_PER_STEP if nc % SSD_CHUNKS_PER_STEP == 0 else 1
    nsteps = nc // cps
    tq = cps * q
    epilogue = y_other is not None
    d = 1 if rev else 0

    def cidx(s):
        return (nsteps - 1 - s) if rev else s

    col_spec = pl.BlockSpec((1, tq, LANES), lambda s: (d, cidx(s), 0))
    in_specs = [
        pl.BlockSpec((tq, SSM_INNER), lambda s: (cidx(s), 1)),
        pl.BlockSpec((tq, SSM_GN), lambda s: (cidx(s), 2 * SSM_INNER // SSM_GN)),
        pl.BlockSpec((tq, SSM_GN), lambda s: (cidx(s), 2 * SSM_INNER // SSM_GN + 1)),
        col_spec,
        pl.BlockSpec((1, cps, LANES, q), lambda s: (d, cidx(s), 0, 0)),
        pl.BlockSpec((1, tq, 4 * LANES), lambda s: (d, cidx(s), 0)),
        pl.BlockSpec((1, cps, 8, LANES), lambda s: (d, cidx(s), 0, 0)),
        pl.BlockSpec((SSM_GROUPS, 2 * LANES, SSM_HPG * SSM_HEAD_DIM), lambda s: (0, 0, 0)),
        pl.BlockSpec((SSM_GROUPS, SSM_STATE, SSM_HPG * SSM_HEAD_DIM), lambda s: (0, 0, 0)),
    ]
    cum2, rowt, ef, etot = prep
    args = [zxbc, zxbc, zxbc, cum2, rowt, ef, etot, _head_select(), h0]
    if epilogue:
        in_specs += [
            pl.BlockSpec((tq, SSM_INNER), lambda s: (cidx(s), 0)),
            pl.BlockSpec((tq, SSM_INNER), lambda s: (cidx(s), 0)),
            pl.BlockSpec((1, SSM_INNER), lambda s: (0, 0)),
            pl.BlockSpec((1, SSM_INNER), lambda s: (0, 0)),
        ]
        args += [zxbc, y_other, d_skip, norm_w]
    state_shape = (SSM_GROUPS, SSM_STATE, SSM_HPG * SSM_HEAD_DIM)
    return pl.pallas_call(
        functools.partial(_ssd_kernel, rev=rev, epilogue=epilogue, cps=cps),
        grid=(nsteps,),
        in_specs=in_specs,
        out_specs=[
            pl.BlockSpec((tq, SSM_INNER), lambda s: (cidx(s), 0)),
            pl.BlockSpec(state_shape, lambda s: (0, 0, 0)),
        ],
        out_shape=[
            jax.ShapeDtypeStruct((rows, SSM_INNER), BF16),
            jax.ShapeDtypeStruct(state_shape, F32),
        ],
        scratch_shapes=[pltpu.VMEM(state_shape, F32)],
        compiler_params=_cparams(("arbitrary",)),
        name="ssd_bwd" if rev else "ssd_fwd",
    )(*args)


def _sgu_kernel(u_ref, v_ref, x_ref, lnw_ref, lnb_ref, ws_ref, bs_ref, wo_ref, g_ref, o_ref,
                vn_scr, uv_scr, *, tm):
    v = v_ref[...].astype(F32)
    mu = jnp.mean(v, axis=-1, keepdims=True)
    vc = v - mu
    var = jnp.mean(vc * vc, axis=-1, keepdims=True)
    vn_scr[...] = (vc * lax.rsqrt(var + EPS) * lnw_ref[...] + lnb_ref[...]).astype(BF16)
    gw = SGU_INNER // SGU_GROUPS
    for ch in range(tm // TOKEN_CHUNK):
        rs = slice(ch * TOKEN_CHUNK, (ch + 1) * TOKEN_CHUNK)
        for g in range(SGU_GROUPS):
            cs = slice(g * gw, (g + 1) * gw)
            sv = jnp.dot(ws_ref[g], vn_scr[rs, cs], preferred_element_type=F32) + bs_ref[:, cs]
            uv_scr[rs, cs] = (u_ref[rs, cs].astype(F32) * sv).astype(BF16)
    y = jnp.dot(uv_scr[...], wo_ref[...], preferred_element_type=F32)
    o_ref[...] = x_ref[...] + g_ref[...] * y


def _sgu_core(z, x, ln_w, ln_b, w_s, bs_exp, w_out, gate):
    rows, d = x.shape
    e = SGU_INNER
    tm = _row_tile(rows, 512)
    return pl.pallas_call(
        functools.partial(_sgu_kernel, tm=tm),
        grid=(rows // tm,),
        in_specs=[
            pl.BlockSpec((tm, e), lambda i: (i, 0)),
            pl.BlockSpec((tm, e), lambda i: (i, 1)),
            pl.BlockSpec((tm, d), lambda i: (i, 0)),
            pl.BlockSpec((1, e), lambda i: (0, 0)),
            pl.BlockSpec((1, e), lambda i: (0, 0)),
            pl.BlockSpec((SGU_GROUPS, TOKEN_CHUNK, TOKEN_CHUNK), lambda i: (0, 0, 0)),
            pl.BlockSpec((TOKEN_CHUNK, e), lambda i: (0, 0)),
            pl.BlockSpec((e, d), lambda i: (0, 0)),
            pl.BlockSpec((1, d), lambda i: (0, 0)),
        ],
        out_specs=pl.BlockSpec((tm, d), lambda i: (i, 0)),
        out_shape=jax.ShapeDtypeStruct((rows, d), F32),
        scratch_shapes=[pltpu.VMEM((tm, e), BF16), pltpu.VMEM((tm, e), BF16)],
        compiler_params=_cparams(("arbitrary",)),
        name="sgu_core",
    )(z, z, x, ln_w, ln_b, w_s, bs_exp, w_out, gate)


def _softmax_pv(s_list, v_list):
    m = s_list[0].max(axis=-1, keepdims=True)
    for s in s_list[1:]:
        m = jnp.maximum(m, s.max(axis=-1, keepdims=True))
    ps = [jnp.exp(s - m) for s in s_list]
    den = sum(p.sum(axis=-1, keepdims=True) for p in ps)
    o = sum(jnp.dot(p.astype(BF16), v, preferred_element_type=F32) for p, v in zip(ps, v_list))
    return o * (1.0 / den)


_NT = (((1,), (1,)), ((), ()))


def _na_kernel(q_ref, k_ref, v_ref, kc_ref, vc_ref, bias_ref, o_ref, *, rows, nblk, nsub):
    kc = kc_ref[...]
    vc = vc_ref[...]
    tq = NA_RB * GRID_W
    lo = _lane_lt((tq, LANES), NA_HEAD_DIM)
    for sb in range(nsub):
        rb = pl.program_id(1) * nsub + sb
        wstart = jnp.clip(rb * NA_RB - NA_ROW_WIN // 2, 0, rows - NA_WR)
        variant = jnp.where(rb == 0, 0, jnp.where(rb == nblk - 1, 2, 1))
        koff = pl.multiple_of(wstart * GRID_W, GRID_W)
        kw = k_ref[pl.ds(koff, NA_WR * GRID_W), :]
        vw = v_ref[pl.ds(koff, NA_WR * GRID_W), :]
        qv = q_ref[sb * tq:(sb + 1) * tq, :]
        zero = jnp.zeros_like(qv)
        outs = []
        for hh in range(2):
            qm = jnp.where(lo, qv, zero) if hh == 0 else jnp.where(lo, zero, qv)
            s_win = lax.dot_general(qm, kw, _NT, preferred_element_type=F32) + bias_ref[hh, variant]
            s_ctx = lax.dot_general(qm, kc, _NT, preferred_element_type=F32)
            outs.append(_softmax_pv([s_win, s_ctx], [vw, vc]))
        o_ref[sb * tq:(sb + 1) * tq, :] = jnp.where(lo, outs[0], outs[1]).astype(o_ref.dtype)


def _na_attention(qkv_lat, qkv_ctx, bias):
    n_lat = qkv_lat.shape[0]
    n_ctx = qkv_ctx.shape[0]
    rows = n_lat // GRID_W
    nblk = rows // NA_RB
    assert rows >= NA_WR and rows % NA_RB == 0
    hp_n = NA_HEADS // 2
    nsub = 8 if nblk % 8 == 0 else 1
    tq = nsub * NA_RB * GRID_W
    return pl.pallas_call(
        functools.partial(_na_kernel, rows=rows, nblk=nblk, nsub=nsub),
        grid=(hp_n, nblk // nsub),
        in_specs=[
            pl.BlockSpec((tq, LANES), lambda hp, rb: (rb, hp)),
            pl.BlockSpec((n_lat, LANES), lambda hp, rb: (0, hp_n + hp)),
            pl.BlockSpec((n_lat, LANES), lambda hp, rb: (0, 2 * hp_n + hp)),
            pl.BlockSpec((n_ctx, LANES), lambda hp, rb: (0, hp_n + hp)),
            pl.BlockSpec((n_ctx, LANES), lambda hp, rb: (0, 2 * hp_n + hp)),
            pl.BlockSpec((2, 3, NA_RB * GRID_W, NA_WR * GRID_W), lambda hp, rb: (hp, 0, 0, 0)),
        ],
        out_specs=pl.BlockSpec((tq, LANES), lambda hp, rb: (rb, hp)),
        out_shape=jax.ShapeDtypeStruct((n_lat, D_MODEL), BF16),
        compiler_params=_cparams(("arbitrary", "arbitrary")),
        name="na_attention",
    )(qkv_lat, qkv_lat, qkv_lat, qkv_ctx, qkv_ctx, bias)


def _ctx_attn_kernel(q_ref, k_ref, v_ref, o_ref):
    qv = q_ref[...]
    lo = _lane_lt(qv.shape, NA_HEAD_DIM)
    zero = jnp.zeros_like(qv)
    outs = []
    for hh in range(2):
        qm = jnp.where(lo, qv, zero) if hh == 0 else jnp.where(lo, zero, qv)
        s = lax.dot_general(qm, k_ref[...], _NT, preferred_element_type=F32)
        outs.append(_softmax_pv([s], [v_ref[...]]))
    o_ref[...] = jnp.where(lo, outs[0], outs[1]).astype(o_ref.dtype)


def _ctx_attention(qkv_ctx):
    n_ctx = qkv_ctx.shape[0]
    hp_n = NA_HEADS // 2
    return pl.pallas_call(
        _ctx_attn_kernel,
        grid=(hp_n,),
        in_specs=[
            pl.BlockSpec((n_ctx, LANES), lambda hp: (0, hp)),
            pl.BlockSpec((n_ctx, LANES), lambda hp: (0, hp_n + hp)),
            pl.BlockSpec((n_ctx, LANES), lambda hp: (0, 2 * hp_n + hp)),
        ],
        out_specs=pl.BlockSpec((n_ctx, LANES), lambda hp: (0, hp)),
        out_shape=jax.ShapeDtypeStruct((n_ctx, D_MODEL), BF16),
        compiler_params=_cparams(("arbitrary",)),
        name="ctx_attention",
    )(qkv_ctx, qkv_ctx, qkv_ctx)


def _na_bias_table(rpb, rows):
    col = np.arange(GRID_W)
    col_start = np.clip(col - NA_COL_WIN // 2, 0, GRID_W - NA_COL_WIN)
    in_win = (col[None, :] >= col_start[:, None]) & (col[None, :] < col_start[:, None] + NA_COL_WIN)
    w = GRID_W
    edge = w - NA_COL_WIN
    rp = jnp.pad(rpb.astype(F32), ((0, 0), (0, 0), (edge, edge)))
    col_bias = jnp.stack([rp[:, :, w - 1 - qc:2 * w - 1 - qc] for qc in range(w)], axis=2)
    col_bias = jnp.where(in_win, col_bias, -jnp.inf)
    cbt = jnp.transpose(col_bias, (0, 2, 1, 3)).reshape(NA_HEADS, w, (2 * NA_ROW_WIN - 1) * w)
    wr = NA_ROW_WIN
    assert rows >= NA_WR
    variants = []
    for r0 in (0, NA_RB, rows - NA_RB):
        wstart = int(np.clip(r0 - NA_ROW_WIN // 2, 0, rows - NA_WR))
        blocks = []
        for qr in range(NA_RB):
            r = r0 + qr
            rs = int(np.clip(r - wr // 2, 0, rows - wr))
            first = rs - wstart
            a0 = rs - r + NA_ROW_WIN - 1
            blk = cbt[:, :, a0 * w:(a0 + wr) * w]
            blocks.append(jnp.pad(blk, ((0, 0), (0, 0), (first * w, (NA_WR - first - wr) * w)),
                                  constant_values=-jnp.inf))
        variants.append(jnp.concatenate(blocks, axis=1))
    return jnp.stack(variants, axis=1)


def _ffn_kernel(xp_ref, x_ref, xn_ref, mul_ref, sh_ref, gate_ref, wup_ref, cw_ref, cb_ref, wdn_ref,
                o_ref, h_scr, acc_scr, act_scr, *, tm, nblk):
    _fill_h_with_halo(h_scr, xp_ref, x_ref, xn_ref, mul_ref[...], sh_ref[...], tm, nblk)
    hc = FFN_CHUNK
    nch = FFN_HIDDEN // hc
    def up(c):
        for half in range(2):
            col = slice(half * FFN_HIDDEN + c * hc, half * FFN_HIDDEN + (c + 1) * hc)
            acc_scr[c, :, half * hc:(half + 1) * hc] = jnp.dot(
                h_scr[...], wup_ref[:, col], preferred_element_type=F32)

    lead = 2
    for c in range(lead):
        up(c)
    for c in range(nch):
        if c + lead < nch:
            up(c + lead)
        a = _dwconv_from_scratch(acc_scr.at[c], cw_ref.at[c], cb_ref.at[c], FFN_CONV, tm)
        act_scr[c] = (_silu(a[:, :hc]) * a[:, hc:]).astype(BF16)
    y = jnp.dot(act_scr[0], wdn_ref[0], preferred_element_type=F32)
    for c in range(1, nch):
        y = y + jnp.dot(act_scr[c], wdn_ref[c], preferred_element_type=F32)
    o_ref[...] = x_ref[...] + gate_ref[...] * y


def _ffn(x, mul, shift, gate, wup, cw_c, cb_c, wdn_c):
    rows, d = x.shape
    nch, _, hc2 = cw_c.shape
    tm = _row_tile(rows, 512)
    nblk = rows // tm
    in_specs = _halo_specs(tm, d, rows) + [
        pl.BlockSpec((1, d), lambda i: (0, 0)),
        pl.BlockSpec((1, d), lambda i: (0, 0)),
        pl.BlockSpec((1, d), lambda i: (0, 0)),
        pl.BlockSpec((d, 2 * FFN_HIDDEN), lambda i: (0, 0), pipeline_mode=pl.Buffered(1)),
        pl.BlockSpec((nch, FFN_CONV, hc2), lambda i: (0, 0, 0)),
        pl.BlockSpec((nch, 1, hc2), lambda i: (0, 0, 0)),
        pl.BlockSpec((nch, hc2 // 2, d), lambda i: (0, 0, 0), pipeline_mode=pl.Buffered(1)),
    ]
    return pl.pallas_call(
        functools.partial(_ffn_kernel, tm=tm, nblk=nblk),
        grid=(nblk,),
        in_specs=in_specs,
        out_specs=pl.BlockSpec((tm, d), lambda i: (i, 0)),
        out_shape=jax.ShapeDtypeStruct((rows, d), F32),
        scratch_shapes=[
            pltpu.VMEM((tm + 2 * HALO, d), BF16),
            pltpu.VMEM((nch, tm + 2 * HALO, hc2), F32),
            pltpu.VMEM((nch, tm, hc2 // 2), BF16),
        ],
        compiler_params=_cparams(("arbitrary",)),
        name="conv_ffn",
    )(x, x, x, mul, shift, gate, wup, cw_c, cb_c, wdn_c)


def _ffn_chunked(w):
    lead = w.shape[:-1]
    nch = FFN_HIDDEN // FFN_CHUNK
    w2 = w.reshape(lead + (2, nch, FFN_CHUNK))
    w2 = jnp.moveaxis(w2, -2, 0)
    return w2.reshape((nch,) + lead + (2 * FFN_CHUNK,))


def _pad_lanes(v, n):
    return jnp.pad(v, [(0, 0)] * (v.ndim - 1) + [(0, n - v.shape[-1])])


def kernel(x, c, ctx, c_ctx, norm_w, w_mod, b_mod, ssm_w_in, ssm_conv_w, ssm_conv_b, ssm_a_log, ssm_dt_bias, ssm_d_skip, ssm_norm_w, ssm_w_out, sgu_w_in, sgu_ln_w, sgu_ln_b, sgu_w_s, sgu_b_s, sgu_w_out, na_w_qkv, na_q_norm, na_k_norm, na_rpb, na_w_out, ffn_w_up, ffn_conv_w, ffn_conv_b, ffn_w_down):
    assert x.shape[0] == 1 and c.shape[0] == 1
    d = D_MODEL
    depth = w_mod.shape[0]
    x_lat = x[0]
    x_ctx = ctx[0]
    cond = jnp.zeros((8, d), F32).at[0].set(c[0]).at[1].set(c_ctx)
    mods = _mod_vectors(cond, w_mod, b_mod)

    def row(v):
        return v.reshape(1, -1)

    for i in range(depth):
        kind, j = i % N_MIXERS, i // N_MIXERS
        need_ctx = i < depth - 1
        ml = [row(mods[i, 0, k * d:(k + 1) * d]) for k in range(6)]
        mc = [row(mods[i, 1, k * d:(k + 1) * d]) for k in range(6)]
        nw0, nw1 = row(norm_w[i, 0]), row(norm_w[i, 1])
        mul_l, mul_c = nw0 * (1.0 + ml[1]), nw0 * (1.0 + mc[1])

        if kind == 0:
            w_in = ssm_w_in[j]
            w_zxbc = w_in[:, :SSM_ZXBC].astype(BF16)
            w_dt = w_in[:, SSM_ZXBC:]
            w_dt = jnp.concatenate([_pad_lanes(w_dt[:, :SSM_HEADS], LANES),
                                    _pad_lanes(w_dt[:, SSM_HEADS:], LANES)], axis=1).astype(BF16)
            cw = jnp.concatenate([jnp.zeros((SSM_CONV, SSM_INNER), F32), ssm_conv_w[j]], axis=1)
            cb = jnp.concatenate([jnp.zeros((SSM_INNER,), F32), ssm_conv_b[j]]).reshape(1, -1)
            a_neg = _pad_lanes(-jnp.exp(ssm_a_log[j]), LANES)
            dtb = _pad_lanes(ssm_dt_bias[j], LANES)
            dsk = row(jnp.repeat(ssm_d_skip[j, 0] + ssm_d_skip[j, 1], SSM_HEAD_DIM))
            gnw = row(ssm_norm_w[j])
            w_out = ssm_w_out[j].astype(BF16)
            h0 = jnp.zeros((SSM_GROUPS, SSM_STATE, SSM_HPG * SSM_HEAD_DIM), F32)

            def mixer(xs, mul, shift, hf0, hb0):
                zxbc, dt_raw = _ssm_in(xs, mul, shift, w_zxbc, w_dt, cw, cb)
                prep = _ssd_prep(dt_raw, dtb, a_neg)
                yf, hf = _ssd_scan(zxbc, prep, hf0, False)
                gn, hb = _ssd_scan(zxbc, prep, hb0, True, y_other=yf, d_skip=dsk, norm_w=gnw)
                return gn, hf, hb

            gn_c, hf, hb = mixer(x_ctx, mul_c, mc[0], h0, h0)
            gn_l, _, _ = mixer(x_lat, mul_l, ml[0], hf, hb)
            x_lat = _mm_res(gn_l, w_out, x_lat, ml[2])
            if need_ctx:
                x_ctx = _mm_res(gn_c, w_out, x_ctx, mc[2])
        elif kind == 1:
            w_in = sgu_w_in[j].astype(BF16)
            w_s = sgu_w_s[j].astype(BF16)
            gw = SGU_INNER // SGU_GROUPS
            bs_exp = jnp.repeat(sgu_b_s[j].T, gw, axis=1)
            w_out = sgu_w_out[j].astype(BF16)
            lnw, lnb = row(sgu_ln_w[j]), row(sgu_ln_b[j])
            z_l = _modmm(x_lat, mul_l, ml[0], w_in, "gelu")
            x_lat = _sgu_core(z_l, x_lat, lnw, lnb, w_s, bs_exp, w_out, ml[2])
            if need_ctx:
                z_c = _modmm(x_ctx, mul_c, mc[0], w_in, "gelu")
                x_ctx = _sgu_core(z_c, x_ctx, lnw, lnb, w_s, bs_exp, w_out, mc[2])
        else:
            w_qkv = na_w_qkv[j].astype(BF16)
            scale = NA_HEAD_DIM ** -0.5
            nw = jnp.concatenate([jnp.tile(na_q_norm[j] * scale, NA_HEADS),
                                  jnp.tile(na_k_norm[j], NA_HEADS),
                                  jnp.ones((d,), F32)]).reshape(1, -1)
            w_out = na_w_out[j].astype(BF16)
            bias = _na_bias_table(na_rpb[j], x_lat.shape[0] // GRID_W)
            qkv_c = _modmm(x_ctx, mul_c, mc[0], w_qkv, "qkv", nw)
            qkv_l = _modmm(x_lat, mul_l, ml[0], w_qkv, "qkv", nw)
            o_l = _na_attention(qkv_l, qkv_c, bias)
            x_lat = _mm_res(o_l, w_out, x_lat, ml[2])
            if need_ctx:
                o_c = _ctx_attention(qkv_c)
                x_ctx = _mm_res(o_c, w_out, x_ctx, mc[2])

        nch = FFN_HIDDEN // FFN_CHUNK
        wup_c = ffn_w_up[i].astype(BF16)
        cw_c = _ffn_chunked(ffn_conv_w[i])
        cb_c = _ffn_chunked(ffn_conv_b[i].reshape(1, -1))
        wdn_c = ffn_w_down[i].astype(BF16).reshape(nch, FFN_CHUNK, d)
        x_lat = _ffn(x_lat, nw1 * (1.0 + ml[4]), ml[3], ml[5], wup_c, cw_c, cb_c, wdn_c)
        if need_ctx:
            x_ctx = _ffn(x_ctx, nw1 * (1.0 + mc[4]), mc[3], mc[5], wup_c, cw_c, cb_c, wdn_c)
    return x_lat[None]
```

```python
import functools
import math

import numpy as np
import jax
import jax.numpy as jnp
from jax import lax
from jax.experimental import pallas as pl
from jax.experimental.pallas import tpu as pltpu

F32 = jnp.float32
BF16 = jnp.bfloat16
HIGHEST = lax.Precision.HIGHEST

D_MODEL = 1024
DEPTH = 4
N_MIXERS = 3
EPS = 1e-6
GRID_W = 64
SSM_INNER = 2 * D_MODEL
SSM_HEAD_DIM = 64
SSM_HEADS = SSM_INNER // SSM_HEAD_DIM
SSM_GROUPS = 8
SSM_HPG = SSM_HEADS // SSM_GROUPS
SSM_STATE = 128
SSM_CONV = 7
SSD_CHUNK = 128
SSM_GN = SSM_GROUPS * SSM_STATE
SSM_ZXBC = 2 * SSM_INNER + 2 * SSM_GN
SGU_INNER = 2 * D_MODEL
SGU_GROUPS = 8
TOKEN_CHUNK = 128
NA_HEAD_DIM = 64
NA_HEADS = D_MODEL // NA_HEAD_DIM
NA_ROW_WIN = 8
NA_COL_WIN = 16
FFN_HIDDEN = 2816
FFN_CONV = 3

LANES = 128
BF16_SUBLANES = 16
VMEM_LIMIT = 56 * 1024 * 1024

HALO = BF16_SUBLANES
FFN_CHUNK = 256
SSD_CHUNKS_PER_STEP = 8
ROW_PITCH = 2
NA_RB = 4
NA_WR = NA_RB + NA_ROW_WIN


def _cparams(sem, flags=None):
    return pltpu.CompilerParams(dimension_semantics=sem, vmem_limit_bytes=VMEM_LIMIT, flags=flags)


def _row_tile(n, pref):
    t = min(n, pref)
    assert n % t == 0
    return t


def _sigmoid(v):
    return 1.0 / (1.0 + jnp.exp(-v))


def _silu(v):
    return v * _sigmoid(v)


def _modulate(x, mul, shift):
    ms = jnp.mean(x * x, axis=-1, keepdims=True)
    return x * lax.rsqrt(ms + EPS) * mul + shift


def _lane_lt(shape, n):
    return lax.broadcasted_iota(jnp.int32, shape, len(shape) - 1) < n


def _mod_kernel(c_ref, w_ref, b_ref, o_ref):
    s = _silu(c_ref[...])
    s_hi = s.astype(BF16)
    s_lo = (s - s_hi.astype(F32)).astype(BF16)
    r = jnp.dot(jnp.concatenate([s_hi, s_lo], axis=0), w_ref[0].astype(BF16), preferred_element_type=F32)
    o_ref[0] = r[0:8] + r[8:16] + b_ref[0]


def _mod_vectors(cond, w_mod, b_mod):
    depth, d, n = w_mod.shape
    tn = 1536
    return pl.pallas_call(
        _mod_kernel,
        grid=(depth, n // tn),
        in_specs=[
            pl.BlockSpec((8, d), lambda i, j: (0, 0)),
            pl.BlockSpec((1, d, tn), lambda i, j: (i, 0, j)),
            pl.BlockSpec((1, 1, tn), lambda i, j: (i, 0, j)),
        ],
        out_specs=pl.BlockSpec((1, 8, tn), lambda i, j: (i, 0, j)),
        out_shape=jax.ShapeDtypeStruct((depth, 8, n), F32),
        compiler_params=_cparams(("arbitrary", "arbitrary")),
        name="mod_vectors",
    )(cond, w_mod, b_mod.reshape(depth, 1, n))


def _gelu_tanh(v):
    c = math.sqrt(2.0 / math.pi)
    return v * (0.5 * (1.0 + jnp.tanh(c * (v + 0.044715 * (v * v * v)))))


def _head_rmsnorm(blk, nw):
    lo = _lane_lt(blk.shape, NA_HEAD_DIM)
    sq = blk * blk
    s_lo = jnp.sum(jnp.where(lo, sq, 0.0), axis=-1, keepdims=True)
    s_hi = jnp.sum(jnp.where(lo, 0.0, sq), axis=-1, keepdims=True)
    r_lo = lax.rsqrt(s_lo * (1.0 / NA_HEAD_DIM) + EPS)
    r_hi = lax.rsqrt(s_hi * (1.0 / NA_HEAD_DIM) + EPS)
    return blk * jnp.where(lo, r_lo, r_hi) * nw


def _modmm_kernel(x_ref, mul_ref, sh_ref, w_ref, *rest, mode, tn):
    if mode == "qkv":
        nw_ref, o_ref, h_scr = rest
    else:
        o_ref, h_scr = rest
    h_scr[...] = _modulate(x_ref[...], mul_ref[...], sh_ref[...]).astype(BF16)
    n = w_ref.shape[1]
    for j in range(n // tn):
        col = slice(j * tn, (j + 1) * tn)
        acc = jnp.dot(h_scr[...], w_ref[:, col], preferred_element_type=F32)
        if mode == "gelu":
            o_ref[:, col] = _gelu_tanh(acc).astype(o_ref.dtype)
        elif j < 2 * D_MODEL // tn:
            for b in range(tn // LANES):
                sl = slice(j * tn + b * LANES, j * tn + (b + 1) * LANES)
                o_ref[:, sl] = _head_rmsnorm(acc[:, b * LANES:(b + 1) * LANES], nw_ref[:, sl]).astype(o_ref.dtype)
        else:
            o_ref[:, col] = acc.astype(o_ref.dtype)


def _modmm(x, mul, shift, w, mode, nw=None):
    rows, d = x.shape
    n = w.shape[1]
    tm = _row_tile(rows, 1024)
    tn = 256
    in_specs = [
        pl.BlockSpec((tm, d), lambda i: (i, 0)),
        pl.BlockSpec((1, d), lambda i: (0, 0)),
        pl.BlockSpec((1, d), lambda i: (0, 0)),
        pl.BlockSpec((d, n), lambda i: (0, 0), pipeline_mode=pl.Buffered(1)),
    ]
    args = [x, mul, shift, w]
    if mode == "qkv":
        in_specs.append(pl.BlockSpec((1, n), lambda i: (0, 0)))
        args.append(nw)
    return pl.pallas_call(
        functools.partial(_modmm_kernel, mode=mode, tn=tn),
        grid=(rows // tm,),
        in_specs=in_specs,
        out_specs=pl.BlockSpec((tm, n), lambda i: (i, 0)),
        out_shape=jax.ShapeDtypeStruct((rows, n), BF16),
        scratch_shapes=[pltpu.VMEM((tm, d), BF16)],
        compiler_params=_cparams(("arbitrary",)),
        name="modmm_" + mode,
    )(*args)


def _mmres_kernel(a_ref, w_ref, x_ref, g_ref, o_ref):
    y = jnp.dot(a_ref[...], w_ref[...], preferred_element_type=F32)
    o_ref[...] = x_ref[...] + g_ref[...] * y


def _mm_res(a, w, x, gate):
    rows, k = a.shape
    d = w.shape[1]
    tm = _row_tile(rows, 1024)
    return pl.pallas_call(
        _mmres_kernel,
        grid=(rows // tm,),
        in_specs=[
            pl.BlockSpec((tm, k), lambda i: (i, 0)),
            pl.BlockSpec((k, d), lambda i: (0, 0)),
            pl.BlockSpec((tm, d), lambda i: (i, 0)),
            pl.BlockSpec((1, d), lambda i: (0, 0)),
        ],
        out_specs=pl.BlockSpec((tm, d), lambda i: (i, 0)),
        out_shape=jax.ShapeDtypeStruct((rows, d), F32),
        compiler_params=_cparams(("arbitrary",)),
        name="mm_res",
    )(a, w, x, gate)


def _fill_h_with_halo(h_scr, xp_ref, x_ref, xn_ref, mul, sh, tm, nblk):
    i = pl.program_id(0)
    h_scr[HALO:HALO + tm, :] = _modulate(x_ref[...], mul, sh).astype(BF16)
    hp = jnp.where(i > 0, _modulate(xp_ref[...], mul, sh), 0.0)
    hn = jnp.where(i < nblk - 1, _modulate(xn_ref[...], mul, sh), 0.0)
    h_scr[0:HALO, :] = hp.astype(BF16)
    h_scr[HALO + tm:HALO + tm + HALO, :] = hn.astype(BF16)


def _halo_specs(tm, d, rows):
    per = tm // HALO
    last = rows // HALO - 1
    return [
        pl.BlockSpec((HALO, d), lambda i, *_: (jnp.maximum(i * per - 1, 0), 0)),
        pl.BlockSpec((tm, d), lambda i, *_: (i, 0)),
        pl.BlockSpec((HALO, d), lambda i, *_: (jnp.minimum((i + 1) * per, last), 0)),
    ]


def _dwconv_from_scratch(acc_scr, cw_ref, cb_ref, taps, tm):
    base = HALO - taps // 2
    y = cb_ref[...] + cw_ref[0:1, :] * acc_scr[pl.ds(base, tm), :]
    for k in range(1, taps):
        y = y + cw_ref[k:k + 1, :] * acc_scr[pl.ds(base + k, tm), :]
    return y


def _dwconv_pitched(a, acc_scr, slab0, cw_ref, cb_ref, col0, taps, tm, finish):
    n, c = a.shape
    for l in range(c // LANES):
        acc_scr[slab0 + l, pl.ds(0, n, stride=ROW_PITCH), :] = a[:, l * LANES:(l + 1) * LANES]
    for l in range(c // LANES):
        lanes = slice(col0 + l * LANES, col0 + (l + 1) * LANES)
        y = cb_ref[:, lanes]
        for k in range(taps):
            off = HALO - taps // 2 + k
            y = y + cw_ref[k:k + 1, lanes] * acc_scr[slab0 + l, pl.ds(ROW_PITCH * off, tm, stride=ROW_PITCH), :]
        finish(lanes, y)


def _ssm_in_kernel(xp_ref, x_ref, xn_ref, mul_ref, sh_ref, w_ref, wdt_ref, cw_ref, cb_ref,
                   o_ref, dt_ref, h_scr, acc_scr, *, tm, nblk, tn):
    _fill_h_with_halo(h_scr, xp_ref, x_ref, xn_ref, mul_ref[...], sh_ref[...], tm, nblk)
    h_mid = h_scr[HALO:HALO + tm, :]
    dt_ref[...] = jnp.dot(h_mid, wdt_ref[...], preferred_element_type=F32)
    def finish(lanes, y):
        o_ref[:, lanes] = _silu(y).astype(o_ref.dtype)

    def z_tile(j):
        col = slice(j * tn, (j + 1) * tn)
        o_ref[:, col] = _silu(jnp.dot(h_mid, w_ref[:, col], preferred_element_type=F32)).astype(o_ref.dtype)

    j0 = SSM_INNER // tn
    n_conv = SSM_ZXBC // tn - j0
    assert n_conv % j0 == 0
    for i in range(n_conv):
        j = j0 + i
        col = slice(j * tn, (j + 1) * tn)
        a = jnp.dot(h_scr[...], w_ref[:, col], preferred_element_type=F32)
        if (i * j0) % n_conv < j0:
            z_tile(i * j0 // n_conv)
        _dwconv_pitched(a, acc_scr, i * (tn // LANES), cw_ref, cb_ref, j * tn, SSM_CONV, tm, finish)


def _ssm_in(x, mul, shift, w_zxbc, w_dt, conv_w, conv_b):
    rows, d = x.shape
    n = w_zxbc.shape[1]
    tm = _row_tile(rows, 512)
    tn = 256
    nblk = rows // tm
    ndt = w_dt.shape[1]
    in_specs = _halo_specs(tm, d, rows) + [
        pl.BlockSpec((1, d), lambda i: (0, 0)),
        pl.BlockSpec((1, d), lambda i: (0, 0)),
        pl.BlockSpec((d, n), lambda i: (0, 0), pipeline_mode=pl.Buffered(1)),
        pl.BlockSpec((d, ndt), lambda i: (0, 0)),
        pl.BlockSpec((SSM_CONV, n), lambda i: (0, 0)),
        pl.BlockSpec((1, n), lambda i: (0, 0)),
    ]
    return pl.pallas_call(
        functools.partial(_ssm_in_kernel, tm=tm, nblk=nblk, tn=tn),
        grid=(nblk,),
        in_specs=in_specs,
        out_specs=[
            pl.BlockSpec((tm, n), lambda i: (i, 0)),
            pl.BlockSpec((tm, ndt), lambda i: (i, 0)),
        ],
        out_shape=[
            jax.ShapeDtypeStruct((rows, n), BF16),
            jax.ShapeDtypeStruct((rows, ndt), F32),
        ],
        scratch_shapes=[
            pltpu.VMEM((tm + 2 * HALO, d), BF16),
            pltpu.VMEM(((n - SSM_INNER) // LANES, ROW_PITCH * (tm + 2 * HALO), LANES), F32),
        ],
        compiler_params=_cparams(("arbitrary",)),
        name="ssm_in",
    )(x, x, x, mul, shift, w_zxbc, w_dt, conv_w, conv_b)


def _softplus(v):
    return jnp.maximum(v, 0.0) + jnp.log1p(jnp.exp(-jnp.abs(v)))


LOG2E = 1.4426950408889634


def _scan_mask(q, rev):
    row = lax.broadcasted_iota(jnp.int32, (q, q), 0)
    col = lax.broadcasted_iota(jnp.int32, (q, q), 1)
    return (col >= row) if rev else (row >= col)


def _hi_lo(v):
    hi = v.astype(BF16)
    return hi, (v - hi.astype(F32)).astype(BF16)


def _ssd_prep_kernel(dt_ref, dtb_ref, an_ref, cum2_ref, rowt_ref, ef_ref, etot_ref, *, nck):
    q = SSD_CHUNK
    for d in range(2):
        rev = d == 1
        tri = jnp.where(_scan_mask(q, rev), 1.0, 0.0).astype(BF16)
        for ck in range(nck):
            rs = slice(ck * q, (ck + 1) * q)
            dt = _softplus(dt_ref[rs, d * LANES:(d + 1) * LANES] + dtb_ref[d:d + 1, :])
            d_a = dt * an_ref[d:d + 1, :]
            p1 = d_a.astype(BF16)
            r1 = d_a - p1.astype(F32)
            p2 = r1.astype(BF16)
            p3 = (r1 - p2.astype(F32)).astype(BF16)
            c3 = jnp.dot(tri, jnp.concatenate([p1, p2, p3], axis=1), preferred_element_type=F32)
            cum = c3[:, 0:LANES] + c3[:, LANES:2 * LANES] + c3[:, 2 * LANES:3 * LANES]
            tot = cum[0:1, :] if rev else cum[q - 1:q, :]
            cum2_ref[d, rs, :] = cum * LOG2E
            rowt_ref[d, ck] = ((cum - jnp.log(dt)) * LOG2E).T
            parts = _hi_lo(jnp.exp(cum)) + _hi_lo(dt * jnp.exp(tot - cum))
            for k, part in enumerate(parts):
                ef_ref[d, rs, k * LANES:(k + 1) * LANES] = part
            etot_ref[d, ck] = jnp.broadcast_to(jnp.exp(tot), (8, LANES))


def _ssd_prep(dt_raw, dt_bias, a_neg):
    rows = dt_raw.shape[0]
    q = SSD_CHUNK
    tm = _row_tile(rows, 1024)
    nck = tm // q
    nc = rows // q
    col_spec = pl.BlockSpec((2, tm, LANES), lambda i: (0, i, 0))
    return pl.pallas_call(
        functools.partial(_ssd_prep_kernel, nck=nck),
        grid=(rows // tm,),
        in_specs=[
            pl.BlockSpec((tm, 2 * LANES), lambda i: (i, 0)),
            pl.BlockSpec((2, LANES), lambda i: (0, 0)),
            pl.BlockSpec((2, LANES), lambda i: (0, 0)),
        ],
        out_specs=[
            col_spec,
            pl.BlockSpec((2, nck, LANES, q), lambda i: (0, i, 0, 0)),
            pl.BlockSpec((2, tm, 4 * LANES), lambda i: (0, i, 0)),
            pl.BlockSpec((2, nck, 8, LANES), lambda i: (0, i, 0, 0)),
        ],
        out_shape=[
            jax.ShapeDtypeStruct((2, rows, LANES), F32),
            jax.ShapeDtypeStruct((2, nc, LANES, q), F32),
            jax.ShapeDtypeStruct((2, rows, 4 * LANES), BF16),
            jax.ShapeDtypeStruct((2, nc, 8, LANES), F32),
        ],
        compiler_params=_cparams(("arbitrary",)),
        name="ssd_prep",
    )(dt_raw, dt_bias, a_neg)


def _ssd_kernel(*refs, rev, epilogue, cps):
    if epilogue:
        (x_ref, b_ref, c_ref, cum2_ref, rowt_ref, ef_ref, etot_ref, sel_ref, h0_ref,
         z_ref, yf_ref, dsk_ref, nw_ref, y_ref, hf_ref, h_scr) = refs
    else:
        (x_ref, b_ref, c_ref, cum2_ref, rowt_ref, ef_ref, etot_ref, sel_ref, h0_ref,
         y_ref, hf_ref, h_scr) = refs
    q = SSD_CHUNK
    hd = SSM_HEAD_DIM
    step = pl.program_id(0)

    @pl.when(step == 0)
    def _():
        h_scr[...] = h0_ref[...]

    tmask = _scan_mask(q, rev)
    lo = _lane_lt((q, LANES), hd)
    lo1 = _lane_lt((1, LANES), hd)
    gw = SSM_HPG * hd
    for sub in (reversed(range(cps)) if rev else range(cps)):
        rows = slice(sub * q, (sub + 1) * q)
        _ssd_chunk(refs, rows, sub, tmask, lo, lo1, gw, epilogue)

    @pl.when(step == pl.num_programs(0) - 1)
    def _():
        hf_ref[...] = h_scr[...]


def _ssd_chunk(refs, rows, sub, tmask, lo, lo1, gw, epilogue):
    if epilogue:
        (x_ref, b_ref, c_ref, cum2_ref, rowt_ref, ef_ref, etot_ref, sel_ref, h0_ref,
         z_ref, yf_ref, dsk_ref, nw_ref, y_ref, hf_ref, h_scr) = refs
    else:
        (x_ref, b_ref, c_ref, cum2_ref, rowt_ref, ef_ref, etot_ref, sel_ref, h0_ref,
         y_ref, hf_ref, h_scr) = refs
    hd = SSM_HEAD_DIM
    cum2 = cum2_ref[0, rows, :]
    rowt = rowt_ref[0, sub]
    etot = etot_ref[0, sub][0:1, :]

    for g in range(SSM_GROUPS):
        bg = b_ref[rows, g * SSM_STATE:(g + 1) * SSM_STATE]
        cg = c_ref[rows, g * SSM_STATE:(g + 1) * SSM_STATE]
        cb = lax.dot_general(cg, bg, (((1,), (1,)), ((), ())), preferred_element_type=F32)
        h_t = h_scr[g]
        esc_g = jnp.dot(ef_ref[0, rows, 0:2 * LANES], sel_ref[g], preferred_element_type=F32)
        fsc_g = jnp.dot(ef_ref[0, rows, 2 * LANES:4 * LANES], sel_ref[g], preferred_element_type=F32)
        yoff = jnp.dot(cg, h_t.astype(BF16), preferred_element_type=F32) * esc_g
        wg = (x_ref[rows, g * gw:(g + 1) * gw].astype(F32) * fsc_g).astype(BF16)
        dec_rows = []
        gated = []
        for pair in range(SSM_HPG // 2):
            ms = []
            for rr in range(2):
                h = g * SSM_HPG + 2 * pair + rr
                seg = cum2[:, h:h + 1] - rowt[h:h + 1, :]
                ms.append((cb * jnp.exp2(jnp.where(tmask, seg, -jnp.inf))).astype(BF16))
            h0i = g * SSM_HPG + 2 * pair
            c0 = g * SSM_HPG * hd + pair * LANES
            xp = x_ref[rows, c0:c0 + LANES]
            zero = jnp.zeros_like(xp)
            xcat = jnp.concatenate([jnp.where(lo, xp, zero), jnp.where(lo, zero, xp)], axis=0)
            ydiag = jnp.dot(jnp.concatenate(ms, axis=1), xcat, preferred_element_type=F32)
            y_pair = ydiag + yoff[:, pair * LANES:(pair + 1) * LANES]
            dec_rows.append(jnp.where(lo1, etot[:, h0i:h0i + 1], etot[:, h0i + 1:h0i + 2]))
            if epilogue:
                yt = y_pair + yf_ref[rows, c0:c0 + LANES].astype(F32) + dsk_ref[:, c0:c0 + LANES] * xp.astype(F32)
                gated.append(yt * z_ref[rows, c0:c0 + LANES].astype(F32))
            else:
                y_ref[rows, c0:c0 + LANES] = y_pair.astype(y_ref.dtype)
        if epilogue:
            ssq = sum(jnp.sum(v * v, axis=-1, keepdims=True) for v in gated)
            rinv = lax.rsqrt(ssq * (1.0 / (SSM_HPG * hd)) + EPS)
            for pair, v in enumerate(gated):
                c0 = g * SSM_HPG * hd + pair * LANES
                y_ref[rows, c0:c0 + LANES] = (v * rinv * nw_ref[:, c0:c0 + LANES]).astype(y_ref.dtype)
        st = lax.dot_general(bg, wg, (((0,), (0,)), ((), ())), preferred_element_type=F32)
        h_scr[g] = h_t * jnp.concatenate(dec_rows, axis=1) + st


def _head_select():
    head = np.arange(LANES)[None, :, None]
    col = np.arange(SSM_HPG * SSM_HEAD_DIM)[None, None, :]
    g = np.arange(SSM_GROUPS)[:, None, None]
    one = (head == g * SSM_HPG + col // SSM_HEAD_DIM).astype(np.float32)
    return jnp.asarray(np.concatenate([one, one], axis=1), dtype=BF16)


def _ssd_scan(zxbc, prep, h0, rev, y_other=None, d_skip=None, norm_w=None):
    rows = zxbc.shape[0]
    q = SSD_CHUNK
    nc = rows // q
    cps = SSD_CHUNKS_PER_STEP if nc % SSD_CHUNKS_PER_STEP == 0 else 1
    nsteps = nc // cps
    tq = cps * q
    epilogue = y_other is not None
    d = 1 if rev else 0

    def cidx(s):
        return (nsteps - 1 - s) if rev else s

    col_spec = pl.BlockSpec((1, tq, LANES), lambda s: (d, cidx(s), 0))
    in_specs = [
        pl.BlockSpec((tq, SSM_INNER), lambda s: (cidx(s), 1)),
        pl.BlockSpec((tq, SSM_GN), lambda s: (cidx(s), 2 * SSM_INNER // SSM_GN)),
        pl.BlockSpec((tq, SSM_GN), lambda s: (cidx(s), 2 * SSM_INNER // SSM_GN + 1)),
        col_spec,
        pl.BlockSpec((1, cps, LANES, q), lambda s: (d, cidx(s), 0, 0)),
        pl.BlockSpec((1, tq, 4 * LANES), lambda s: (d, cidx(s), 0)),
        pl.BlockSpec((1, cps, 8, LANES), lambda s: (d, cidx(s), 0, 0)),
        pl.BlockSpec((SSM_GROUPS, 2 * LANES, SSM_HPG * SSM_HEAD_DIM), lambda s: (0, 0, 0)),
        pl.BlockSpec((SSM_GROUPS, SSM_STATE, SSM_HPG * SSM_HEAD_DIM), lambda s: (0, 0, 0)),
    ]
    cum2, rowt, ef, etot = prep
    args = [zxbc, zxbc, zxbc, cum2, rowt, ef, etot, _head_select(), h0]
    if epilogue:
        in_specs += [
            pl.BlockSpec((tq, SSM_INNER), lambda s: (cidx(s), 0)),
            pl.BlockSpec((tq, SSM_INNER), lambda s: (cidx(s), 0)),
            pl.BlockSpec((1, SSM_INNER), lambda s: (0, 0)),
            pl.BlockSpec((1, SSM_INNER), lambda s: (0, 0)),
        ]
        args += [zxbc, y_other, d_skip, norm_w]
    state_shape = (SSM_GROUPS, SSM_STATE, SSM_HPG * SSM_HEAD_DIM)
    return pl.pallas_call(
        functools.partial(_ssd_kernel, rev=rev, epilogue=epilogue, cps=cps),
        grid=(nsteps,),
        in_specs=in_specs,
        out_specs=[
            pl.BlockSpec((tq, SSM_INNER), lambda s: (cidx(s), 0)),
            pl.BlockSpec(state_shape, lambda s: (0, 0, 0)),
        ],
        out_shape=[
            jax.ShapeDtypeStruct((rows, SSM_INNER), BF16),
            jax.ShapeDtypeStruct(state_shape, F32),
        ],
        scratch_shapes=[pltpu.VMEM(state_shape, F32)],
        compiler_params=_cparams(("arbitrary",)),
        name="ssd_bwd" if rev else "ssd_fwd",
    )(*args)


def _sgu_kernel(u_ref, v_ref, x_ref, lnw_ref, lnb_ref, ws_ref, bs_ref, wo_ref, g_ref, o_ref,
                vn_scr, uv_scr, *, tm):
    v = v_ref[...].astype(F32)
    mu = jnp.mean(v, axis=-1, keepdims=True)
    vc = v - mu
    var = jnp.mean(vc * vc, axis=-1, keepdims=True)
    vn_scr[...] = (vc * lax.rsqrt(var + EPS) * lnw_ref[...] + lnb_ref[...]).astype(BF16)
    gw = SGU_INNER // SGU_GROUPS
    for ch in range(tm // TOKEN_CHUNK):
        rs = slice(ch * TOKEN_CHUNK, (ch + 1) * TOKEN_CHUNK)
        for g in range(SGU_GROUPS):
            cs = slice(g * gw, (g + 1) * gw)
            sv = jnp.dot(ws_ref[g], vn_scr[rs, cs], preferred_element_type=F32) + bs_ref[:, cs]
            uv_scr[rs, cs] = (u_ref[rs, cs].astype(F32) * sv).astype(BF16)
    y = jnp.dot(uv_scr[...], wo_ref[...], preferred_element_type=F32)
    o_ref[...] = x_ref[...] + g_ref[...] * y


def _sgu_core(z, x, ln_w, ln_b, w_s, bs_exp, w_out, gate):
    rows, d = x.shape
    e = SGU_INNER
    tm = _row_tile(rows, 512)
    return pl.pallas_call(
        functools.partial(_sgu_kernel, tm=tm),
        grid=(rows // tm,),
        in_specs=[
            pl.BlockSpec((tm, e), lambda i: (i, 0)),
            pl.BlockSpec((tm, e), lambda i: (i, 1)),
            pl.BlockSpec((tm, d), lambda i: (i, 0)),
            pl.BlockSpec((1, e), lambda i: (0, 0)),
            pl.BlockSpec((1, e), lambda i: (0, 0)),
            pl.BlockSpec((SGU_GROUPS, TOKEN_CHUNK, TOKEN_CHUNK), lambda i: (0, 0, 0)),
            pl.BlockSpec((TOKEN_CHUNK, e), lambda i: (0, 0)),
            pl.BlockSpec((e, d), lambda i: (0, 0)),
            pl.BlockSpec((1, d), lambda i: (0, 0)),
        ],
        out_specs=pl.BlockSpec((tm, d), lambda i: (i, 0)),
        out_shape=jax.ShapeDtypeStruct((rows, d), F32),
        scratch_shapes=[pltpu.VMEM((tm, e), BF16), pltpu.VMEM((tm, e), BF16)],
        compiler_params=_cparams(("arbitrary",)),
        name="sgu_core",
    )(z, z, x, ln_w, ln_b, w_s, bs_exp, w_out, gate)


def _softmax_pv(s_list, v_list):
    m = s_list[0].max(axis=-1, keepdims=True)
    for s in s_list[1:]:
        m = jnp.maximum(m, s.max(axis=-1, keepdims=True))
    ps = [jnp.exp(s - m) for s in s_list]
    den = sum(p.sum(axis=-1, keepdims=True) for p in ps)
    o = sum(jnp.dot(p.astype(BF16), v, preferred_element_type=F32) for p, v in zip(ps, v_list))
    return o * (1.0 / den)


_NT = (((1,), (1,)), ((), ()))


def _na_kernel(q_ref, k_ref, v_ref, kc_ref, vc_ref, bias_ref, o_ref, *, rows, nblk, nsub):
    kc = kc_ref[...]
    vc = vc_ref[...]
    tq = NA_RB * GRID_W
    lo = _lane_lt((tq, LANES), NA_HEAD_DIM)
    for sb in range(nsub):
        rb = pl.program_id(1) * nsub + sb
        wstart = jnp.clip(rb * NA_RB - NA_ROW_WIN // 2, 0, rows - NA_WR)
        variant = jnp.where(rb == 0, 0, jnp.where(rb == nblk - 1, 2, 1))
        koff = pl.multiple_of(wstart * GRID_W, GRID_W)
        kw = k_ref[pl.ds(koff, NA_WR * GRID_W), :]
        vw = v_ref[pl.ds(koff, NA_WR * GRID_W), :]
        qv = q_ref[sb * tq:(sb + 1) * tq, :]
        zero = jnp.zeros_like(qv)
        outs = []
        for hh in range(2):
            qm = jnp.where(lo, qv, zero) if hh == 0 else jnp.where(lo, zero, qv)
            s_win = lax.dot_general(qm, kw, _NT, preferred_element_type=F32) + bias_ref[hh, variant]
            s_ctx = lax.dot_general(qm, kc, _NT, preferred_element_type=F32)
            outs.append(_softmax_pv([s_win, s_ctx], [vw, vc]))
        o_ref[sb * tq:(sb + 1) * tq, :] = jnp.where(lo, outs[0], outs[1]).astype(o_ref.dtype)


def _na_attention(qkv_lat, qkv_ctx, bias):
    n_lat = qkv_lat.shape[0]
    n_ctx = qkv_ctx.shape[0]
    rows = n_lat // GRID_W
    nblk = rows // NA_RB
    assert rows >= NA_WR and rows % NA_RB == 0
    hp_n = NA_HEADS // 2
    nsub = 8 if nblk % 8 == 0 else 1
    tq = nsub * NA_RB * GRID_W
    return pl.pallas_call(
        functools.partial(_na_kernel, rows=rows, nblk=nblk, nsub=nsub),
        grid=(hp_n, nblk // nsub),
        in_specs=[
            pl.BlockSpec((tq, LANES), lambda hp, rb: (rb, hp)),
            pl.BlockSpec((n_lat, LANES), lambda hp, rb: (0, hp_n + hp)),
            pl.BlockSpec((n_lat, LANES), lambda hp, rb: (0, 2 * hp_n + hp)),
            pl.BlockSpec((n_ctx, LANES), lambda hp, rb: (0, hp_n + hp)),
            pl.BlockSpec((n_ctx, LANES), lambda hp, rb: (0, 2 * hp_n + hp)),
            pl.BlockSpec((2, 3, NA_RB * GRID_W, NA_WR * GRID_W), lambda hp, rb: (hp, 0, 0, 0)),
        ],
        out_specs=pl.BlockSpec((tq, LANES), lambda hp, rb: (rb, hp)),
        out_shape=jax.ShapeDtypeStruct((n_lat, D_MODEL), BF16),
        compiler_params=_cparams(("arbitrary", "arbitrary")),
        name="na_attention",
    )(qkv_lat, qkv_lat, qkv_lat, qkv_ctx, qkv_ctx, bias)


def _ctx_attn_kernel(q_ref, k_ref, v_ref, o_ref):
    qv = q_ref[...]
    lo = _lane_lt(qv.shape, NA_HEAD_DIM)
    zero = jnp.zeros_like(qv)
    outs = []
    for hh in range(2):
        qm = jnp.where(lo, qv, zero) if hh == 0 else jnp.where(lo, zero, qv)
        s = lax.dot_general(qm, k_ref[...], _NT, preferred_element_type=F32)
        outs.append(_softmax_pv([s], [v_ref[...]]))
    o_ref[...] = jnp.where(lo, outs[0], outs[1]).astype(o_ref.dtype)


def _ctx_attention(qkv_ctx):
    n_ctx = qkv_ctx.shape[0]
    hp_n = NA_HEADS // 2
    return pl.pallas_call(
        _ctx_attn_kernel,
        grid=(hp_n,),
        in_specs=[
            pl.BlockSpec((n_ctx, LANES), lambda hp: (0, hp)),
            pl.BlockSpec((n_ctx, LANES), lambda hp: (0, hp_n + hp)),
            pl.BlockSpec((n_ctx, LANES), lambda hp: (0, 2 * hp_n + hp)),
        ],
        out_specs=pl.BlockSpec((n_ctx, LANES), lambda hp: (0, hp)),
        out_shape=jax.ShapeDtypeStruct((n_ctx, D_MODEL), BF16),
        compiler_params=_cparams(("arbitrary",)),
        name="ctx_attention",
    )(qkv_ctx, qkv_ctx, qkv_ctx)


def _na_bias_table(rpb, rows):
    col = np.arange(GRID_W)
    col_start = np.clip(col - NA_COL_WIN // 2, 0, GRID_W - NA_COL_WIN)
    in_win = (col[None, :] >= col_start[:, None]) & (col[None, :] < col_start[:, None] + NA_COL_WIN)
    w = GRID_W
    edge = w - NA_COL_WIN
    rp = jnp.pad(rpb.astype(F32), ((0, 0), (0, 0), (edge, edge)))
    col_bias = jnp.stack([rp[:, :, w - 1 - qc:2 * w - 1 - qc] for qc in range(w)], axis=2)
    col_bias = jnp.where(in_win, col_bias, -jnp.inf)
    cbt = jnp.transpose(col_bias, (0, 2, 1, 3)).reshape(NA_HEADS, w, (2 * NA_ROW_WIN - 1) * w)
    wr = NA_ROW_WIN
    assert rows >= NA_WR
    variants = []
    for r0 in (0, NA_RB, rows - NA_RB):
        wstart = int(np.clip(r0 - NA_ROW_WIN // 2, 0, rows - NA_WR))
        blocks = []
        for qr in range(NA_RB):
            r = r0 + qr
            rs = int(np.clip(r - wr // 2, 0, rows - wr))
            first = rs - wstart
            a0 = rs - r + NA_ROW_WIN - 1
            blk = cbt[:, :, a0 * w:(a0 + wr) * w]
            blocks.append(jnp.pad(blk, ((0, 0), (0, 0), (first * w, (NA_WR - first - wr) * w)),
                                  constant_values=-jnp.inf))
        variants.append(jnp.concatenate(blocks, axis=1))
    return jnp.stack(variants, axis=1)


def _ffn_kernel(xp_ref, x_ref, xn_ref, mul_ref, sh_ref, gate_ref, wup_ref, cw_ref, cb_ref, wdn_ref,
                o_ref, h_scr, acc_scr, act_scr, *, tm, nblk):
    _fill_h_with_halo(h_scr, xp_ref, x_ref, xn_ref, mul_ref[...], sh_ref[...], tm, nblk)
    hc = FFN_CHUNK
    nch = FFN_HIDDEN // hc
    def up(c):
        for half in range(2):
            col = slice(half * FFN_HIDDEN + c * hc, half * FFN_HIDDEN + (c + 1) * hc)
            acc_scr[c, :, half * hc:(half + 1) * hc] = jnp.dot(
                h_scr[...], wup_ref[:, col], preferred_element_type=F32)

    lead = 2
    for c in range(lead):
        up(c)
    for c in range(nch):
        if c + lead < nch:
            up(c + lead)
        a = _dwconv_from_scratch(acc_scr.at[c], cw_ref.at[c], cb_ref.at[c], FFN_CONV, tm)
        act_scr[c] = (_silu(a[:, :hc]) * a[:, hc:]).astype(BF16)
    y = jnp.dot(act_scr[0], wdn_ref[0], preferred_element_type=F32)
    for c in range(1, nch):
        y = y + jnp.dot(act_scr[c], wdn_ref[c], preferred_element_type=F32)
    o_ref[...] = x_ref[...] + gate_ref[...] * y


def _ffn(x, mul, shift, gate, wup, cw_c, cb_c, wdn_c):
    rows, d = x.shape
    nch, _, hc2 = cw_c.shape
    tm = _row_tile(rows, 512)
    nblk = rows // tm
    in_specs = _halo_specs(tm, d, rows) + [
        pl.BlockSpec((1, d), lambda i: (0, 0)),
        pl.BlockSpec((1, d), lambda i: (0, 0)),
        pl.BlockSpec((1, d), lambda i: (0, 0)),
        pl.BlockSpec((d, 2 * FFN_HIDDEN), lambda i: (0, 0), pipeline_mode=pl.Buffered(1)),
        pl.BlockSpec((nch, FFN_CONV, hc2), lambda i: (0, 0, 0)),
        pl.BlockSpec((nch, 1, hc2), lambda i: (0, 0, 0)),
        pl.BlockSpec((nch, hc2 // 2, d), lambda i: (0, 0, 0), pipeline_mode=pl.Buffered(1)),
    ]
    return pl.pallas_call(
        functools.partial(_ffn_kernel, tm=tm, nblk=nblk),
        grid=(nblk,),
        in_specs=in_specs,
        out_specs=pl.BlockSpec((tm, d), lambda i: (i, 0)),
        out_shape=jax.ShapeDtypeStruct((rows, d), F32),
        scratch_shapes=[
            pltpu.VMEM((tm + 2 * HALO, d), BF16),
            pltpu.VMEM((nch, tm + 2 * HALO, hc2), F32),
            pltpu.VMEM((nch, tm, hc2 // 2), BF16),
        ],
        compiler_params=_cparams(("arbitrary",)),
        name="conv_ffn",
    )(x, x, x, mul, shift, gate, wup, cw_c, cb_c, wdn_c)


def _ffn_chunked(w):
    lead = w.shape[:-1]
    nch = FFN_HIDDEN // FFN_CHUNK
    w2 = w.reshape(lead + (2, nch, FFN_CHUNK))
    w2 = jnp.moveaxis(w2, -2, 0)
    return w2.reshape((nch,) + lead + (2 * FFN_CHUNK,))


def _pad_lanes(v, n):
    return jnp.pad(v, [(0, 0)] * (v.ndim - 1) + [(0, n - v.shape[-1])])


def kernel(x, c, ctx, c_ctx, norm_w, w_mod, b_mod, ssm_w_in, ssm_conv_w, ssm_conv_b, ssm_a_log, ssm_dt_bias, ssm_d_skip, ssm_norm_w, ssm_w_out, sgu_w_in, sgu_ln_w, sgu_ln_b, sgu_w_s, sgu_b_s, sgu_w_out, na_w_qkv, na_q_norm, na_k_norm, na_rpb, na_w_out, ffn_w_up, ffn_conv_w, ffn_conv_b, ffn_w_down):
    assert x.shape[0] == 1 and c.shape[0] == 1
    d = D_MODEL
    depth = w_mod.shape[0]
    x_lat = x[0]
    x_ctx = ctx[0]
    cond = jnp.zeros((8, d), F32).at[0].set(c[0]).at[1].set(c_ctx)
    mods = _mod_vectors(cond, w_mod, b_mod)

    def row(v):
        return v.reshape(1, -1)

    for i in range(depth):
        kind, j = i % N_MIXERS, i // N_MIXERS
        need_ctx = i < depth - 1
        ml = [row(mods[i, 0, k * d:(k + 1) * d]) for k in range(6)]
        mc = [row(mods[i, 1, k * d:(k + 1) * d]) for k in range(6)]
        nw0, nw1 = row(norm_w[i, 0]), row(norm_w[i, 1])
        mul_l, mul_c = nw0 * (1.0 + ml[1]), nw0 * (1.0 + mc[1])

        if kind == 0:
            w_in = ssm_w_in[j]
            w_zxbc = w_in[:, :SSM_ZXBC].astype(BF16)
            w_dt = w_in[:, SSM_ZXBC:]
            w_dt = jnp.concatenate([_pad_lanes(w_dt[:, :SSM_HEADS], LANES),
                                    _pad_lanes(w_dt[:, SSM_HEADS:], LANES)], axis=1).astype(BF16)
            cw = jnp.concatenate([jnp.zeros((SSM_CONV, SSM_INNER), F32), ssm_conv_w[j]], axis=1)
            cb = jnp.concatenate([jnp.zeros((SSM_INNER,), F32), ssm_conv_b[j]]).reshape(1, -1)
            a_neg = _pad_lanes(-jnp.exp(ssm_a_log[j]), LANES)
            dtb = _pad_lanes(ssm_dt_bias[j], LANES)
            dsk = row(jnp.repeat(ssm_d_skip[j, 0] + ssm_d_skip[j, 1], SSM_HEAD_DIM))
            gnw = row(ssm_norm_w[j])
            w_out = ssm_w_out[j].astype(BF16)
            h0 = jnp.zeros((SSM_GROUPS, SSM_STATE, SSM_HPG * SSM_HEAD_DIM), F32)

            def mixer(xs, mul, shift, hf0, hb0):
                zxbc, dt_raw = _ssm_in(xs, mul, shift, w_zxbc, w_dt, cw, cb)
                prep = _ssd_prep(dt_raw, dtb, a_neg)
                yf, hf = _ssd_scan(zxbc, prep, hf0, False)
                gn, hb = _ssd_scan(zxbc, prep, hb0, True, y_other=yf, d_skip=dsk, norm_w=gnw)
                return gn, hf, hb

            gn_c, hf, hb = mixer(x_ctx, mul_c, mc[0], h0, h0)
            gn_l, _, _ = mixer(x_lat, mul_l, ml[0], hf, hb)
            x_lat = _mm_res(gn_l, w_out, x_lat, ml[2])
            if need_ctx:
                x_ctx = _mm_res(gn_c, w_out, x_ctx, mc[2])
        elif kind == 1:
            w_in = sgu_w_in[j].astype(BF16)
            w_s = sgu_w_s[j].astype(BF16)
            gw = SGU_INNER // SGU_GROUPS
            bs_exp = jnp.repeat(sgu_b_s[j].T, gw, axis=1)
            w_out = sgu_w_out[j].astype(BF16)
            lnw, lnb = row(sgu_ln_w[j]), row(sgu_ln_b[j])
            z_l = _modmm(x_lat, mul_l, ml[0], w_in, "gelu")
            x_lat = _sgu_core(z_l, x_lat, lnw, lnb, w_s, bs_exp, w_out, ml[2])
            if need_ctx:
                z_c = _modmm(x_ctx, mul_c, mc[0], w_in, "gelu")
                x_ctx = _sgu_core(z_c, x_ctx, lnw, lnb, w_s, bs_exp, w_out, mc[2])
        else:
            w_qkv = na_w_qkv[j].astype(BF16)
            scale = NA_HEAD_DIM ** -0.5
            nw = jnp.concatenate([jnp.tile(na_q_norm[j] * scale, NA_HEADS),
                                  jnp.tile(na_k_norm[j], NA_HEADS),
                                  jnp.ones((d,), F32)]).reshape(1, -1)
            w_out = na_w_out[j].astype(BF16)
            bias = _na_bias_table(na_rpb[j], x_lat.shape[0] // GRID_W)
            qkv_c = _modmm(x_ctx, mul_c, mc[0], w_qkv, "qkv", nw)
            qkv_l = _modmm(x_lat, mul_l, ml[0], w_qkv, "qkv", nw)
            o_l = _na_attention(qkv_l, qkv_c, bias)
            x_lat = _mm_res(o_l, w_out, x_lat, ml[2])
            if need_ctx:
                o_c = _ctx_attention(qkv_c)
                x_ctx = _mm_res(o_c, w_out, x_ctx, mc[2])

        nch = FFN_HIDDEN // FFN_CHUNK
        wup_c = ffn_w_up[i].astype(BF16)
        cw_c = _ffn_chunked(ffn_conv_w[i])
        cb_c = _ffn_chunked(ffn_conv_b[i].reshape(1, -1))
        wdn_c = ffn_w_down[i].astype(BF16).reshape(nch, FFN_CHUNK, d)
        x_lat = _ffn(x_lat, nw1 * (1.0 + ml[4]), ml[3], ml[5], wup_c, cw_c, cb_c, wdn_c)
        if need_ctx:
            x_ctx = _ffn(x_ctx, nw1 * (1.0 + mc[4]), mc[3], mc[5], wup_c, cw_c, cb_c, wdn_c)
    return x_lat[None]
```
